```python
import jax, jax.numpy as jnp
from jax import lax
import numpy as np

D_MODEL = 2048
BATCH = 4
SEQ = 4096
DEPTH = 4

N_HEADS = 16
HEAD_DIM = 128
D_LIN = N_HEADS * HEAD_DIM
CONV_QKV = 4
CHUNK = 64
POOL_WINDOWS = (2, 4, 8, 16)
N_POOL_GROUPS = len(POOL_WINDOWS)
D_POOL = D_MODEL
POOL_GROUP_DIM = D_POOL // N_POOL_GROUPS
D_FF = 5632
CONV_FFN = 3
EPS = 1e-6
D_IN_PROJ = 4 * D_LIN + 2 * N_HEADS + D_POOL + 2 * D_MODEL

kernel_name = "hybrid_gdn_pool_convglu_trunk"


def rmsnorm(x, w):
    xf = x.astype(jnp.float32)
    y = xf * lax.rsqrt(jnp.mean(xf * xf, axis=-1, keepdims=True) + EPS)
    return (y * w.astype(jnp.float32)).astype(x.dtype)


def l2norm(x):
    xf = x.astype(jnp.float32)
    return xf * lax.rsqrt(jnp.sum(xf * xf, axis=-1, keepdims=True) + EPS)


def causal_dwconv(x, w):
    width = w.shape[0]
    seq = x.shape[1]
    xp = jnp.pad(x, ((0, 0), (width - 1, 0), (0, 0)))
    y = xp[:, 0:seq] * w[0]
    for j in range(1, width):
        y = y + xp[:, j:j + seq] * w[j]
    return y


def gated_delta_rule(q, k, v, g, beta):
    f32 = jnp.float32
    q, k, v, g, beta = (t.astype(f32) for t in (q, k, v, g, beta))
    bsz, seq, nh, dk = q.shape
    dv = v.shape[-1]
    pad = (-seq) % CHUNK
    if pad:
        q, k, v = (jnp.pad(t, ((0, 0), (0, pad), (0, 0), (0, 0))) for t in (q, k, v))
        g, beta = (jnp.pad(t, ((0, 0), (0, pad), (0, 0))) for t in (g, beta))
    n_chunks = (seq + pad) // CHUNK

    def to_chunks(t):
        return t.reshape(bsz, n_chunks, CHUNK, nh, t.shape[-1]).transpose(0, 3, 1, 2, 4)

    q, k, v = to_chunks(q), to_chunks(k), to_chunks(v)
    g = g.reshape(bsz, n_chunks, CHUNK, nh).transpose(0, 3, 1, 2)
    beta = beta.reshape(bsz, n_chunks, CHUNK, nh).transpose(0, 3, 1, 2)
    g = jnp.cumsum(g, axis=-1)

    tril = jnp.tril(jnp.ones((CHUNK, CHUNK), dtype=bool))
    strict = jnp.tril(jnp.ones((CHUNK, CHUNK), dtype=bool), k=-1)
    diff = g[..., :, None] - g[..., None, :]
    decay = jnp.exp(jnp.where(tril, diff, -jnp.inf))

    k_beta = k * beta[..., None]
    v_beta = v * beta[..., None]
    lmat = jnp.where(strict, jnp.einsum('bhnid,bhnjd->bhnij', k_beta, k) * decay, 0.0)
    amat = lmat + jnp.eye(CHUNK, dtype=f32)
    rhs = jnp.concatenate([v_beta, k_beta * jnp.exp(g)[..., None]], axis=-1)
    sol = lax.linalg.triangular_solve(amat, rhs, left_side=True, lower=True, unit_diagonal=True)
    u_val = sol[..., :dv]
    w_cum = sol[..., dv:]

    attn_intra = jnp.einsum('bhnid,bhnjd->bhnij', q, k) * decay
    q_decay = q * jnp.exp(g)[..., None]
    k_to_end = k * jnp.exp(g[..., -1:] - g)[..., None]
    chunk_decay = jnp.exp(g[..., -1])

    xs = tuple(jnp.moveaxis(t, 2, 0) for t in (u_val, w_cum, q_decay, attn_intra, k_to_end, chunk_decay))

    def step(state, inp):
        u_c, w_c, qd_c, at_c, ke_c, dec_c = inp
        v_new = u_c - jnp.einsum('bhcd,bhdv->bhcv', w_c, state)
        o_c = jnp.einsum('bhcd,bhdv->bhcv', qd_c, state) + jnp.einsum('bhij,bhjv->bhiv', at_c, v_new)
        state = state * dec_c[..., None, None] + jnp.einsum('bhcd,bhcv->bhdv', ke_c, v_new)
        return state, o_c

    state0 = jnp.zeros((bsz, nh, dk, dv), dtype=f32)
    _, o = lax.scan(step, state0, xs)
    o = o.transpose(1, 0, 3, 2, 4).reshape(bsz, n_chunks * CHUNK, nh, dv)
    return o[:, :seq]


def causal_multiscale_pool(u):
    seq = u.shape[1]
    uf = u.astype(jnp.float32)
    cs = jnp.pad(jnp.cumsum(uf, axis=1), ((0, 0), (1, 0), (0, 0), (0, 0)))
    t = jnp.arange(seq)
    outs = []
    for gi, win in enumerate(POOL_WINDOWS):
        hi = cs[:, 1:, gi]
        lo = cs[:, jnp.maximum(t + 1 - win, 0), gi]
        cnt = jnp.minimum(t + 1, win).astype(jnp.float32)
        outs.append((hi - lo) / cnt[None, :, None])
    pooled = jnp.stack(outs, axis=2)
    return (pooled - uf).astype(u.dtype)


def setup_inputs(seed: int = 0) -> dict:
    key = jax.random.key(seed)
    ks = jax.random.split(key, 20)
    f32 = jnp.float32

    def nrm(k, shape, scale):
        return jax.random.normal(k, shape, f32) * scale

    def gain(k, shape):
        return 1.0 + 0.02 * jax.random.normal(k, shape, f32)

    x = jax.random.normal(ks[0], (BATCH, SEQ, D_MODEL), f32)
    a_init = jax.random.uniform(ks[4], (DEPTH, N_HEADS), f32, 1.0, 16.0)
    dt = jnp.exp(jax.random.uniform(ks[5], (DEPTH, N_HEADS), f32, np.log(1e-3), np.log(1e-1)))
    dt_bias = dt + jnp.log(-jnp.expm1(-dt))
    return {
        "x": x,
        "norm_mix_w": gain(ks[1], (DEPTH, D_MODEL)),
        "w_in": nrm(ks[2], (DEPTH, D_MODEL, D_IN_PROJ), D_MODEL ** -0.5),
        "conv_qkv_w": nrm(ks[3], (DEPTH, CONV_QKV, 3 * D_LIN), CONV_QKV ** -0.5),
        "a_log": jnp.log(a_init),
        "dt_bias": dt_bias,
        "gdn_norm_w": gain(ks[6], (DEPTH, HEAD_DIM)),
        "pool_w": nrm(ks[7], (DEPTH, N_POOL_GROUPS, POOL_GROUP_DIM, POOL_GROUP_DIM), POOL_GROUP_DIM ** -0.5),
        "pool_scale": gain(ks[8], (DEPTH, D_POOL)),
        "w_out": nrm(ks[9], (DEPTH, D_MODEL, D_MODEL), D_MODEL ** -0.5),
        "norm_ffn_w": gain(ks[10], (DEPTH, D_MODEL)),
        "w_up": nrm(ks[11], (DEPTH, D_MODEL, 2 * D_FF), D_MODEL ** -0.5),
        "conv_ffn_w": nrm(ks[12], (DEPTH, CONV_FFN, D_FF), CONV_FFN ** -0.5),
        "conv_ffn_b": nrm(ks[13], (DEPTH, D_FF), 0.02),
        "w_down": nrm(ks[14], (DEPTH, D_FF, D_MODEL), D_FF ** -0.5),
        "norm_final_w": gain(ks[15], (D_MODEL,)),
    }


def reference(x, norm_mix_w, w_in, conv_qkv_w, a_log, dt_bias, gdn_norm_w, pool_w, pool_scale,
              w_out, norm_ffn_w, w_up, conv_ffn_w, conv_ffn_b, w_down, norm_final_w):
    bsz, seq, _ = x.shape
    splits = np.cumsum([D_LIN, D_LIN, D_LIN, D_LIN, N_HEADS, N_HEADS, D_POOL, D_MODEL]).tolist()
    for l in range(DEPTH):
        h = rmsnorm(x, norm_mix_w[l])
        proj = jnp.einsum('bsd,de->bse', h, w_in[l])
        q, k, v, z, b_raw, a_raw, p_in, g_a, g_b = jnp.split(proj, splits, axis=-1)

        qkv = jax.nn.silu(causal_dwconv(jnp.concatenate([q, k, v], axis=-1), conv_qkv_w[l]))
        q, k, v = jnp.split(qkv, 3, axis=-1)
        q = l2norm(q.reshape(bsz, seq, N_HEADS, HEAD_DIM)) * (HEAD_DIM ** -0.5)
        k = l2norm(k.reshape(bsz, seq, N_HEADS, HEAD_DIM))
        v = v.reshape(bsz, seq, N_HEADS, HEAD_DIM)
        beta = jax.nn.sigmoid(b_raw.astype(jnp.float32))
        g_log = -jnp.exp(a_log[l].astype(jnp.float32)) * jax.nn.softplus(
            a_raw.astype(jnp.float32) + dt_bias[l].astype(jnp.float32))
        o = gated_delta_rule(q, k, v, g_log, beta).astype(x.dtype)
        o = rmsnorm(o, gdn_norm_w[l]) * jax.nn.silu(z.reshape(bsz, seq, N_HEADS, HEAD_DIM))
        y_a = o.reshape(bsz, seq, D_LIN)

        pooled = causal_multiscale_pool(p_in.reshape(bsz, seq, N_POOL_GROUPS, POOL_GROUP_DIM))
        y_b = jnp.einsum('bsgc,gcd->bsgd', pooled, pool_w[l]).reshape(bsz, seq, D_POOL) * pool_scale[l]

        mixed = jax.nn.sigmoid(g_a) * y_a + jax.nn.sigmoid(g_b) * y_b
        x = x + jnp.einsum('bsd,de->bse', mixed, w_out[l])

        h = rmsnorm(x, norm_ffn_w[l])
        gate, up = jnp.split(jnp.einsum('bsd,df->bsf', h, w_up[l]), 2, axis=-1)
        gate = causal_dwconv(gate, conv_ffn_w[l]) + conv_ffn_b[l]
        x = x + jnp.einsum('bsf,fd->bsd', jax.nn.gelu(gate, approximate=False) * up, w_down[l])
    return rmsnorm(x, norm_final_w)
```

```python
import functools

import jax
import jax.numpy as jnp
from jax import lax
from jax.experimental import pallas as pl
from jax.experimental.pallas import tpu as pltpu

F32 = jnp.float32
BF16 = jnp.bfloat16

HEAD_DIM = 128
CHUNK = 64
GDN_ROWS = 256
CHUNKS_PER_STEP = GDN_ROWS // CHUNK
CONV_QKV_TAPS = 4
CONV_FFN_TAPS = 3
POOL_WINDOWS = (2, 4, 8, 16)
POOL_HALO = 16
SUBLANES = 8
N_GATE_COLS = 128
EPS = 1e-6
VMEM_LIMIT_BYTES = 60 * 1024 * 1024

_NT = (((1,), (1,)), ((), ()))
_TN = (((0,), (0,)), ((), ()))


def _params(semantics):
    return pltpu.CompilerParams(dimension_semantics=semantics, vmem_limit_bytes=VMEM_LIMIT_BYTES)


def _pick_tile(n, want):
    t = min(n, want)
    while n % t:
        t //= 2
    return t


def _rms(x, w):
    return x * lax.rsqrt(jnp.mean(x * x, axis=-1, keepdims=True) + EPS) * w


def _silu(x):
    return x * jax.nn.sigmoid(x)


def _softplus(x):
    return jnp.maximum(x, 0.0) + jnp.log(1.0 + jnp.exp(-jnp.abs(x)))


def _inproj_body(x_ref, nw_ref, w_ref, wg_ref, wgt_ref, o_ref, g_ref, gt_ref, h_ref):
    @pl.when(pl.program_id(1) == 0)
    def _():
        h = _rms(x_ref[...], nw_ref[...]).astype(BF16)
        h_ref[...] = h
        g_ref[...] = jnp.dot(h, wg_ref[...], preferred_element_type=F32)
        gt_ref[...] = lax.dot_general(wgt_ref[...], h, _NT, preferred_element_type=F32)

    o_ref[...] = jnp.dot(h_ref[...], w_ref[...], preferred_element_type=F32).astype(BF16)


def _inproj(x2, norm_w, w_main, w_gate, w_gate_t, *, tm, tn):
    t, d = x2.shape
    nm = w_main.shape[1]
    ng = w_gate_t.shape[0]
    return pl.pallas_call(
        _inproj_body,
        grid=(t // tm, nm // tn),
        in_specs=[
            pl.BlockSpec((tm, d), lambda i, j: (i, 0)),
            pl.BlockSpec((1, d), lambda i, j: (0, 0)),
            pl.BlockSpec((d, tn), lambda i, j: (0, j)),
            pl.BlockSpec((d, N_GATE_COLS), lambda i, j: (0, 0)),
            pl.BlockSpec((ng, d), lambda i, j: (0, 0)),
        ],
        out_specs=[
            pl.BlockSpec((tm, tn), lambda i, j: (i, j)),
            pl.BlockSpec((tm, N_GATE_COLS), lambda i, j: (i, 0)),
            pl.BlockSpec((ng, tm), lambda i, j: (0, i)),
        ],
        out_shape=[
            jax.ShapeDtypeStruct((t, nm), BF16),
            jax.ShapeDtypeStruct((t, N_GATE_COLS), F32),
            jax.ShapeDtypeStruct((ng, t), F32),
        ],
        scratch_shapes=[pltpu.VMEM((tm, d), BF16)],
        compiler_params=_params(("arbitrary", "arbitrary")),
        name="inproj",
    )(x2, norm_w, w_main, w_gate, w_gate_t)


def _segmented_cumsum(x, axis, reverse=False):
    n = x.shape[axis]
    pos = lax.broadcasted_iota(jnp.int32, x.shape, axis) % CHUNK
    shift = 1
    while shift < CHUNK:
        if reverse:
            x = x + jnp.where(pos < CHUNK - shift, pltpu.roll(x, n - shift, axis), 0.0)
        else:
            x = x + jnp.where(pos >= shift, pltpu.roll(x, shift, axis), 0.0)
        shift *= 2
    return x


def _delta_rule_block(qc, kc, vc, beta_c, gcc, gtail, gcr, s):
    r, c, nc = GDN_ROWS, CHUNK, CHUNKS_PER_STEP
    qn = qc * (lax.rsqrt(jnp.sum(qc * qc, axis=-1, keepdims=True) + EPS) * (HEAD_DIM ** -0.5))
    kn = kc * lax.rsqrt(jnp.sum(kc * kc, axis=-1, keepdims=True) + EPS)

    row = lax.broadcasted_iota(jnp.int32, (r, r), 0)
    col = lax.broadcasted_iota(jnp.int32, (r, r), 1)
    same_chunk = (row // c) == (col // c)
    causal = same_chunk & (row >= col)

    eg = jnp.exp(gcc)
    ee = jnp.exp(gtail)

    decay = jnp.where(causal, jnp.exp(jnp.where(causal, gcc - gcr, 0.0)), 0.0)

    kq = jnp.concatenate([kn, qn], axis=0).astype(BF16)
    gram = lax.dot_general(kq, kn.astype(BF16), _NT, preferred_element_type=F32)
    attn = gram[r:] * decay
    n_bd = jnp.where(row > col, gram[:r] * decay, 0.0) * (-beta_c)

    def pack(x_bd):
        out = x_bd[0:c]
        for i in range(1, nc):
            out = out + x_bd[i * c:(i + 1) * c]
        return out

    def unpack(x_cat):
        return jnp.where(same_chunk, jnp.concatenate([x_cat] * nc, axis=0), 0.0)

    n_cat = pack(n_bd)
    eye_cat = (lax.broadcasted_iota(jnp.int32, (c, r), 0)
               == lax.broadcasted_iota(jnp.int32, (c, r), 1) % c).astype(F32)
    p = eye_cat + n_cat
    w_cat = jnp.dot(n_cat.astype(BF16), n_bd.astype(BF16), preferred_element_type=F32)
    power = 2
    while power < c // 2:
        w_bd = unpack(w_cat).astype(BF16)
        x = jnp.dot(jnp.concatenate([p, w_cat], axis=0).astype(BF16), w_bd, preferred_element_type=F32)
        p = p + x[:c]
        w_cat = x[c:]
        power *= 2
    p = p + jnp.dot(p.astype(BF16), unpack(w_cat).astype(BF16), preferred_element_type=F32)
    t_bd = unpack(p).astype(BF16)

    rhs = jnp.concatenate([vc * beta_c, kn * (beta_c * eg)], axis=1).astype(BF16)
    uw = jnp.dot(t_bd, rhs, preferred_element_type=F32).astype(BF16)
    auw = jnp.dot(attn.astype(BF16), uw, preferred_element_type=F32)
    q_eff = qn * eg - auw[:, HEAD_DIM:]
    o_intra = auw[:, :HEAD_DIM]
    ke = (kn * ee).astype(BF16)

    outs = []
    for i in range(nc):
        rows = slice(i * c, (i + 1) * c)
        eb = lax.dot_general(ke[rows], uw[rows], _TN, preferred_element_type=F32)
        lhs = jnp.concatenate([eb[:, HEAD_DIM:], q_eff[rows]], axis=0).astype(BF16)
        po = jnp.dot(lhs, s.astype(BF16), preferred_element_type=F32)
        outs.append(po[HEAD_DIM:] + o_intra[rows])
        dec = jnp.exp(gcr[:, i * c + c - 1:i * c + c])
        s = dec * s - po[:HEAD_DIM] + eb[:, :HEAD_DIM]
    return jnp.concatenate(outs, axis=0), s


def _gdn_body(q_ref, k_ref, v_ref, z_ref, g_ref, gt_ref, cwq_ref, cwk_ref, cwv_ref,
              al_ref, dt_ref, alr_ref, dtr_ref, nw_ref, o_ref,
              pad_ref, s_ref, grow_ref, *, heads_per_step, n_heads):
    r = GDN_ROWS
    first = pl.program_id(2) == 0

    @pl.when(first)
    def _():
        pad_ref[:, 0:SUBLANES, :] = jnp.zeros((3, SUBLANES, pad_ref.shape[2]), F32)
        s_ref[...] = jnp.zeros(s_ref.shape, F32)

    graw = g_ref[...]
    beta_all = jax.nn.sigmoid(graw)
    g_all = -jnp.exp(al_ref[...]) * _softplus(graw + dt_ref[...])
    gc_all = _segmented_cumsum(g_all, 0)
    gtail_all = _segmented_cumsum(g_all, 0, reverse=True) - g_all
    gtraw = gt_ref[...]
    reps = r // N_GATE_COLS
    alr = jnp.concatenate([alr_ref[...]] * reps, axis=1)
    dtr = jnp.concatenate([dtr_ref[...]] * reps, axis=1)
    grow_ref[...] = _segmented_cumsum(-jnp.exp(alr) * _softplus(gtraw + dtr), 1)

    conv = []
    for idx, (x_ref, cw_ref) in enumerate(((q_ref, cwq_ref), (k_ref, cwk_ref), (v_ref, cwv_ref))):
        pad_ref[idx, SUBLANES:SUBLANES + r, :] = x_ref[...].astype(F32)
        cw = cw_ref[...]
        base = SUBLANES - (CONV_QKV_TAPS - 1)
        y = cw[0:1] * pad_ref[idx, pl.ds(base, r), :]
        for j in range(1, CONV_QKV_TAPS):
            y = y + cw[j:j + 1] * pad_ref[idx, pl.ds(base + j, r), :]
        pad_ref[idx, 0:SUBLANES, :] = pad_ref[idx, r:r + SUBLANES, :]
        conv.append(_silu(y))

    lane = lax.broadcasted_iota(jnp.int32, (r, N_GATE_COLS), 1)
    nw = nw_ref[...]
    for hh in range(heads_per_step):
        head = pl.program_id(1) * heads_per_step + hh
        cols = slice(hh * HEAD_DIM, (hh + 1) * HEAD_DIM)
        beta_c = jnp.sum(jnp.where(lane == head, beta_all, 0.0), axis=1, keepdims=True)
        gcc = jnp.sum(jnp.where(lane == head + n_heads, gc_all, 0.0), axis=1, keepdims=True)
        gtail = jnp.sum(jnp.where(lane == head + n_heads, gtail_all, 0.0), axis=1, keepdims=True)
        gcr = grow_ref[pl.ds(head + n_heads, 1), :]
        o, s_new = _delta_rule_block(conv[0][:, cols], conv[1][:, cols], conv[2][:, cols],
                                     beta_c, gcc, gtail, gcr, s_ref[hh])
        s_ref[hh] = s_new
        o_ref[:, cols] = (_rms(o, nw) * _silu(z_ref[:, cols].astype(F32))).astype(BF16)


def _gdn(proj, gates, gates_t, conv_w, al, dt, alr, dtr, norm_w, *, batch, seq, d_lin, heads_per_step):
    t = proj.shape[0]
    n_heads = d_lin // HEAD_DIM
    w = heads_per_step * HEAD_DIM
    nblk = d_lin // w
    r = GDN_ROWS
    steps = seq // r
    row_map = lambda b, g, i: b * steps + i
    ng = gates_t.shape[0]

    def col_spec(offset_blocks):
        return pl.BlockSpec((r, w), lambda b, g, i: (row_map(b, g, i), offset_blocks + g))

    def cw_spec(offset_blocks):
        return pl.BlockSpec((CONV_QKV_TAPS, w), lambda b, g, i: (0, offset_blocks + g))

    small = lambda shape: pl.BlockSpec(shape, lambda b, g, i: (0, 0))
    body = functools.partial(_gdn_body, heads_per_step=heads_per_step, n_heads=n_heads)
    return pl.pallas_call(
        body,
        grid=(batch, nblk, steps),
        in_specs=[
            col_spec(0), col_spec(nblk), col_spec(2 * nblk), col_spec(3 * nblk),
            pl.BlockSpec((r, N_GATE_COLS), lambda b, g, i: (row_map(b, g, i), 0)),
            pl.BlockSpec((ng, r), lambda b, g, i: (0, row_map(b, g, i))),
            cw_spec(0), cw_spec(nblk), cw_spec(2 * nblk),
            small((1, N_GATE_COLS)), small((1, N_GATE_COLS)),
            small((ng, N_GATE_COLS)), small((ng, N_GATE_COLS)),
            small((1, HEAD_DIM)),
        ],
        out_specs=pl.BlockSpec((r, w), lambda b, g, i: (row_map(b, g, i), g)),
        out_shape=jax.ShapeDtypeStruct((t, d_lin), BF16),
        scratch_shapes=[
            pltpu.VMEM((3, r + SUBLANES, w), F32),
            pltpu.VMEM((heads_per_step, HEAD_DIM, HEAD_DIM), F32),
            pltpu.VMEM((ng, r), F32),
        ],
        compiler_params=_params(("arbitrary", "arbitrary", "arbitrary")),
        name="gdn",
    )(proj, proj, proj, proj, gates, gates_t, conv_w, conv_w, conv_w, al, dt, alr, dtr, norm_w)


def _mix_body(p_ref, ga_ref, gb_ref, ya_ref, x_ref, pw_ref, ps_ref, wo_ref, o_ref,
              pad_ref, mixed_ref, *, tiles_per_seq):
    tm = x_ref.shape[0]
    gdim = pw_ref.shape[1]
    seq_tile = pl.program_id(0) % tiles_per_seq

    @pl.when(seq_tile == 0)
    def _():
        pad_ref[0:POOL_HALO, :] = jnp.zeros((POOL_HALO, pad_ref.shape[1]), F32)

    pad_ref[POOL_HALO:POOL_HALO + tm, :] = p_ref[...].astype(F32)
    pos = seq_tile * tm + lax.broadcasted_iota(jnp.int32, (tm, 1), 0)
    for g, win in enumerate(POOL_WINDOWS):
        cols = slice(g * gdim, (g + 1) * gdim)
        u = pad_ref[POOL_HALO:POOL_HALO + tm, cols]
        acc = u
        for sft in range(1, win):
            acc = acc + pad_ref[pl.ds(POOL_HALO - sft, tm), cols]
        cnt = jnp.minimum(pos + 1, win).astype(F32)
        pooled = (acc / cnt - u).astype(BF16)
        yb = jnp.dot(pooled, pw_ref[g], preferred_element_type=F32) * ps_ref[:, cols]
        mixed = (jax.nn.sigmoid(ga_ref[:, cols].astype(F32)) * ya_ref[:, cols].astype(F32)
                 + jax.nn.sigmoid(gb_ref[:, cols].astype(F32)) * yb)
        mixed_ref[:, cols] = mixed.astype(BF16)
    pad_ref[0:POOL_HALO, :] = pad_ref[tm:tm + POOL_HALO, :]
    o_ref[...] = x_ref[...] + jnp.dot(mixed_ref[...], wo_ref[...], preferred_element_type=F32)


def _mix(proj, ya, x2, pool_w, pool_scale, w_out, *, seq, tm):
    t, d = x2.shape
    ngroups, gdim, _ = pool_w.shape
    first_blk = proj.shape[1] // d - 3
    row = lambda i: (i, 0)
    const1 = pl.Buffered(1)
    return pl.pallas_call(
        functools.partial(_mix_body, tiles_per_seq=seq // tm),
        grid=(t // tm,),
        in_specs=[
            pl.BlockSpec((tm, d), lambda i: (i, first_blk)),
            pl.BlockSpec((tm, d), lambda i: (i, first_blk + 1)),
            pl.BlockSpec((tm, d), lambda i: (i, first_blk + 2)),
            pl.BlockSpec((tm, d), row),
            pl.BlockSpec((tm, d), row),
            pl.BlockSpec((ngroups, gdim, gdim), lambda i: (0, 0, 0), pipeline_mode=const1),
            pl.BlockSpec((1, d), lambda i: (0, 0), pipeline_mode=const1),
            pl.BlockSpec((d, d), lambda i: (0, 0), pipeline_mode=const1),
        ],
        out_specs=pl.BlockSpec((tm, d), row),
        out_shape=jax.ShapeDtypeStruct((t, d), F32),
        scratch_shapes=[pltpu.VMEM((tm + POOL_HALO, d), F32), pltpu.VMEM((tm, d), BF16)],
        compiler_params=_params(("arbitrary",)),
        name="mix",
    )(proj, proj, proj, ya, x2, pool_w, pool_scale, w_out)


def _ffn_body(x_ref, nw_ref, wg_ref, wu_ref, cw_ref, cb_ref, wd_ref, o_ref,
              h_ref, pad_ref, carry_ref, *, tiles_per_seq):
    tm = x_ref.shape[0]
    j = pl.program_id(1)
    seq_start = pl.program_id(0) % tiles_per_seq == 0

    @pl.when(j == 0)
    def _():
        h_ref[...] = _rms(x_ref[...], nw_ref[...]).astype(BF16)

    h = h_ref[...]
    gate = jnp.dot(h, wg_ref[...], preferred_element_type=F32)
    up = jnp.dot(h, wu_ref[...], preferred_element_type=F32)

    @pl.when(seq_start)
    def _():
        pad_ref[0:SUBLANES, :] = jnp.zeros((SUBLANES, pad_ref.shape[1]), F32)

    @pl.when(jnp.logical_not(seq_start))
    def _():
        pad_ref[0:SUBLANES, :] = carry_ref[j]

    pad_ref[SUBLANES:SUBLANES + tm, :] = gate
    carry_ref[j] = gate[tm - SUBLANES:]
    cw = cw_ref[...]
    base = SUBLANES - (CONV_FFN_TAPS - 1)
    conv = cb_ref[...] + cw[0:1] * pad_ref[pl.ds(base, tm), :]
    for k in range(1, CONV_FFN_TAPS):
        conv = conv + cw[k:k + 1] * pad_ref[pl.ds(base + k, tm), :]
    act = 0.5 * conv * (1.0 + lax.erf(conv * (2.0 ** -0.5))) * up
    down = jnp.dot(act.astype(BF16), wd_ref[...], preferred_element_type=F32)

    @pl.when(j == 0)
    def _():
        o_ref[...] = x_ref[...] + down

    @pl.when(j > 0)
    def _():
        o_ref[...] += down


def _ffn(x2, norm_w, w_up, conv_w, conv_b, w_down, *, seq, tm, tf):
    t, d = x2.shape
    dff = w_down.shape[0]
    nf = dff // tf
    return pl.pallas_call(
        functools.partial(_ffn_body, tiles_per_seq=seq // tm),
        grid=(t // tm, nf),
        in_specs=[
            pl.BlockSpec((tm, d), lambda i, j: (i, 0)),
            pl.BlockSpec((1, d), lambda i, j: (0, 0)),
            pl.BlockSpec((d, tf), lambda i, j: (0, j)),
            pl.BlockSpec((d, tf), lambda i, j: (0, nf + j)),
            pl.BlockSpec((CONV_FFN_TAPS, tf), lambda i, j: (0, j)),
            pl.BlockSpec((1, tf), lambda i, j: (0, j)),
            pl.BlockSpec((tf, d), lambda i, j: (j, 0)),
        ],
        out_specs=pl.BlockSpec((tm, d), lambda i, j: (i, 0)),
        out_shape=jax.ShapeDtypeStruct((t, d), F32),
        scratch_shapes=[
            pltpu.VMEM((tm, d), BF16),
            pltpu.VMEM((tm + SUBLANES, tf), F32),
            pltpu.VMEM((nf, SUBLANES, tf), F32),
        ],
        compiler_params=_params(("arbitrary", "arbitrary")),
        name="ffn",
    )(x2, norm_w, w_up, w_up, conv_w, conv_b, w_down)


def _final_norm_body(x_ref, w_ref, o_ref):
    o_ref[...] = _rms(x_ref[...], w_ref[...])


def _final_norm(x2, w, *, tm):
    t, d = x2.shape
    return pl.pallas_call(
        _final_norm_body,
        grid=(t // tm,),
        in_specs=[pl.BlockSpec((tm, d), lambda i: (i, 0)), pl.BlockSpec((1, d), lambda i: (0, 0))],
        out_specs=pl.BlockSpec((tm, d), lambda i: (i, 0)),
        out_shape=jax.ShapeDtypeStruct((t, d), F32),
        compiler_params=_params(("arbitrary",)),
        name="final_norm",
    )(x2, w)


def _tile_plan(seq, d, dff):
    return dict(
        inproj_tm=_pick_tile(seq, 1024), inproj_tn=_pick_tile(7 * d, 1024),
        mix_tm=_pick_tile(seq, 512),
        ffn_tm=_pick_tile(seq, 512), ffn_tf=_pick_tile(dff, 512),
        norm_tm=_pick_tile(seq, 512),
        heads_per_step=min(2, d // HEAD_DIM),
    )


def kernel(x, norm_mix_w, w_in, conv_qkv_w, a_log, dt_bias, gdn_norm_w, pool_w, pool_scale, w_out,
           norm_ffn_w, w_up, conv_ffn_w, conv_ffn_b, w_down, norm_final_w):
    bsz, seq, d = x.shape
    depth = w_in.shape[0]
    n_heads = d // HEAD_DIM
    dff = w_down.shape[1]
    assert seq % GDN_ROWS == 0 and d % HEAD_DIM == 0 and 2 * n_heads <= N_GATE_COLS
    assert w_in.shape[2] == 7 * d + 2 * n_heads
    plan = _tile_plan(seq, d, dff)
    x2 = x.reshape(bsz * seq, d)
    gate_lo, gate_hi = 4 * d, 4 * d + 2 * n_heads
    lane_pad = N_GATE_COLS - 2 * n_heads

    for l in range(depth):
        wl = w_in[l]
        w_main = jnp.concatenate([wl[:, :gate_lo], wl[:, gate_hi:]], axis=1).astype(BF16)
        w_gate = wl[:, gate_lo:gate_hi].astype(BF16)
        w_gate_t = w_gate.T
        w_gate = jnp.pad(w_gate, ((0, 0), (0, lane_pad)))
        al = jnp.pad(a_log[l], (n_heads, lane_pad)).reshape(1, N_GATE_COLS)
        dt = jnp.pad(dt_bias[l], (n_heads, lane_pad)).reshape(1, N_GATE_COLS)
        alr = jnp.broadcast_to(jnp.pad(a_log[l], (n_heads, 0))[:, None], (2 * n_heads, N_GATE_COLS))
        dtr = jnp.broadcast_to(jnp.pad(dt_bias[l], (n_heads, 0))[:, None], (2 * n_heads, N_GATE_COLS))

        proj, gates, gates_t = _inproj(x2, norm_mix_w[l].reshape(1, d), w_main, w_gate, w_gate_t,
                                       tm=plan["inproj_tm"], tn=plan["inproj_tn"])
        ya = _gdn(proj, gates, gates_t, conv_qkv_w[l], al, dt, alr, dtr, gdn_norm_w[l].reshape(1, HEAD_DIM),
                  batch=bsz, seq=seq, d_lin=d, heads_per_step=plan["heads_per_step"])
        x2 = _mix(proj, ya, x2, pool_w[l].astype(BF16), pool_scale[l].reshape(1, d), w_out[l].astype(BF16),
                  seq=seq, tm=plan["mix_tm"])
        x2 = _ffn(x2, norm_ffn_w[l].reshape(1, d), w_up[l].astype(BF16), conv_ffn_w[l],
                  conv_ffn_b[l].reshape(1, dff), w_down[l].astype(BF16),
                  seq=seq, tm=plan["ffn_tm"], tf=plan["ffn_tf"])
    y = _final_norm(x2, norm_final_w.reshape(1, d), tm=plan["norm_tm"])
    return y.reshape(bsz, seq, d)
```

```python
import functools

import jax
import jax.numpy as jnp
from jax import lax
from jax.experimental import pallas as pl
from jax.experimental.pallas import tpu as pltpu

F32 = jnp.float32
BF16 = jnp.bfloat16

HEAD_DIM = 128
LANES = 128
CHUNK = 64
GDN_ROWS = 256
CHUNKS_PER_STEP = GDN_ROWS // CHUNK
CONV_QKV_TAPS = 4
CONV_FFN_TAPS = 3
POOL_WINDOWS = (2, 4, 8, 16)
POOL_HALO = 16
SUBLANES = 8
N_GATE_COLS = 128
EPS = 1e-6
MASKED_LOG_DECAY = -1e30
VMEM_LIMIT_BYTES = 60 * 1024 * 1024

_NT = (((1,), (1,)), ((), ()))
_TN = (((0,), (0,)), ((), ()))


def _params(semantics):
    return pltpu.CompilerParams(dimension_semantics=semantics, vmem_limit_bytes=VMEM_LIMIT_BYTES)


def _pick_tile(n, want):
    t = min(n, want)
    while n % t:
        t //= 2
    return t


def _rms(x, w):
    return x * lax.rsqrt(jnp.mean(x * x, axis=-1, keepdims=True) + EPS) * w


def _silu(x):
    return x * jax.nn.sigmoid(x)


def _softplus(x):
    return jnp.maximum(x, 0.0) + jnp.log(1.0 + jnp.exp(-jnp.abs(x)))


def _inproj_body(x_ref, nw_ref, w_ref, wg_ref, cw_ref, o_ref, g_ref, gt_ref,
                 h_ref, pad_ref, carry_ref, *, tiles_per_seq, q_tiles):
    i, j = pl.program_id(0), pl.program_id(1)
    tm, tn = o_ref.shape

    @pl.when(j == 0)
    def _():
        h = _rms(x_ref[...], nw_ref[...]).astype(BF16)
        h_ref[...] = h
        g = jnp.dot(h, wg_ref[...], preferred_element_type=F32)
        g_ref[...] = g
        gt_ref[...] = g.T[:gt_ref.shape[0]]

    acc = jnp.dot(h_ref[...], w_ref[...], preferred_element_type=F32)

    @pl.when(j >= 3 * q_tiles)
    def _():
        o_ref[...] = acc.astype(BF16)

    @pl.when(j < 3 * q_tiles)
    def _():
        seq_start = i % tiles_per_seq == 0
        heads = tn // HEAD_DIM

        @pl.when(seq_start)
        def _():
            pad_ref[:, 0:SUBLANES, :] = jnp.zeros((heads, SUBLANES, HEAD_DIM), F32)

        @pl.when(jnp.logical_not(seq_start))
        def _():
            for hh in range(heads):
                pad_ref[hh, 0:SUBLANES, :] = carry_ref[j, :, hh * HEAD_DIM:(hh + 1) * HEAD_DIM]

        carry_ref[j] = acc[tm - SUBLANES:]
        is_v = j >= 2 * q_tiles
        scale = jnp.where(j < q_tiles, HEAD_DIM ** -0.5, 1.0)
        base = SUBLANES - (CONV_QKV_TAPS - 1)
        for hh in range(heads):
            cols = slice(hh * HEAD_DIM, (hh + 1) * HEAD_DIM)
            pad_ref[hh, SUBLANES:SUBLANES + tm, :] = acc[:, cols]
            y = cw_ref[0:1, cols] * pad_ref[hh, pl.ds(base, tm), :]
            for t in range(1, CONV_QKV_TAPS):
                y = y + cw_ref[t:t + 1, cols] * pad_ref[hh, pl.ds(base + t, tm), :]
            y = _silu(y)
            inv = lax.rsqrt(jnp.sum(y * y, axis=-1, keepdims=True) + EPS) * scale
            o_ref[:, cols] = (y * jnp.where(is_v, 1.0, inv)).astype(BF16)


def _inproj(x2, norm_w, w_main, w_gate, conv_w, *, n_gate_rows, seq, tm, tn):
    t, d = x2.shape
    nm = w_main.shape[1]
    ng = n_gate_rows
    assert d % tn == 0
    q_tiles = d // tn
    body = functools.partial(_inproj_body, tiles_per_seq=seq // tm, q_tiles=q_tiles)
    return pl.pallas_call(
        body,
        grid=(t // tm, nm // tn),
        in_specs=[
            pl.BlockSpec((tm, d), lambda i, j: (i, 0)),
            pl.BlockSpec((1, d), lambda i, j: (0, 0)),
            pl.BlockSpec((d, tn), lambda i, j: (0, j)),
            pl.BlockSpec((d, N_GATE_COLS), lambda i, j: (0, 0)),
            pl.BlockSpec((CONV_QKV_TAPS, tn), lambda i, j: (0, jnp.minimum(j, 3 * q_tiles - 1))),
        ],
        out_specs=[
            pl.BlockSpec((tm, tn), lambda i, j: (i, j)),
            pl.BlockSpec((tm, N_GATE_COLS), lambda i, j: (i, 0)),
            pl.BlockSpec((ng, tm), lambda i, j: (0, i)),
        ],
        out_shape=[
            jax.ShapeDtypeStruct((t, nm), BF16),
            jax.ShapeDtypeStruct((t, N_GATE_COLS), F32),
            jax.ShapeDtypeStruct((ng, t), F32),
        ],
        scratch_shapes=[
            pltpu.VMEM((tm, d), BF16),
            pltpu.VMEM((tn // HEAD_DIM, tm + SUBLANES, HEAD_DIM), F32),
            pltpu.VMEM((3 * q_tiles, SUBLANES, tn), F32),
        ],
        compiler_params=_params(("arbitrary", "arbitrary")),
        name="inproj",
    )(x2, norm_w, w_main, w_gate, conv_w)


def _segmented_cumsum(x, axis, reverse=False):
    n = x.shape[axis]
    pos = lax.broadcasted_iota(jnp.int32, x.shape, axis) % CHUNK
    shift = 1
    while shift < CHUNK:
        if reverse:
            x = x + jnp.where(pos < CHUNK - shift, pltpu.roll(x, n - shift, axis), 0.0)
        else:
            x = x + jnp.where(pos >= shift, pltpu.roll(x, shift, axis), 0.0)
        shift *= 2
    return x


def _delta_rule_heads(heads, states):
    r, c, nc = GDN_ROWS, CHUNK, CHUNKS_PER_STEP
    row = lax.broadcasted_iota(jnp.int32, (r, r), 0)
    col = lax.broadcasted_iota(jnp.int32, (r, r), 1)
    same_chunk = (row // c) == (col // c)
    causal = same_chunk & (row >= col)
    strict = same_chunk & (row > col)
    eye_cat = (lax.broadcasted_iota(jnp.int32, (c, r), 0)
               == lax.broadcasted_iota(jnp.int32, (c, r), 1) % c).astype(F32)
    chunk_mask = same_chunk.astype(F32).astype(BF16)

    def pack(x_bd):
        out = x_bd[0:c]
        for i in range(1, nc):
            out = out + x_bd[i * c:(i + 1) * c]
        return out

    def unpack(x_cat):
        return jnp.concatenate([x_cat.astype(BF16)] * nc, axis=0) * chunk_mask

    def dot(a, b):
        return jnp.dot(a, b, preferred_element_type=F32)

    qns, kns, attns, n_bds = [], [], [], []
    for qn_b, kn_b, vc, beta_c, gcc, gtail, gcr in heads:
        qn = qn_b.astype(F32)
        kn = kn_b.astype(F32)
        diff = gcc - gcr
        decay = jnp.exp(jnp.where(causal, diff, MASKED_LOG_DECAY))
        decay_strict = jnp.exp(jnp.where(strict, diff, MASKED_LOG_DECAY))
        kq = jnp.concatenate([(kn * (-beta_c)).astype(BF16), qn_b], axis=0)
        gram = lax.dot_general(kq, kn_b, _NT, preferred_element_type=F32)
        qns.append(qn)
        kns.append(kn)
        attns.append((gram[r:] * decay).astype(BF16))
        n_bds.append(gram[:r] * decay_strict)

    n_cats = [pack(n_bd) for n_bd in n_bds]
    ps = [eye_cat + n_cat for n_cat in n_cats]
    w_cats = [dot(n_cat.astype(BF16), n_bd.astype(BF16)) for n_cat, n_bd in zip(n_cats, n_bds)]
    power = 2
    while power < c // 2:
        xs = [dot(jnp.concatenate([p, w_cat], axis=0).astype(BF16), unpack(w_cat))
              for p, w_cat in zip(ps, w_cats)]
        ps = [p + x[:c] for p, x in zip(ps, xs)]
        w_cats = [x[c:] for x in xs]
        power *= 2
    ps = [p + dot(p.astype(BF16), unpack(w_cat)) for p, w_cat in zip(ps, w_cats)]

    uws = []
    for (qn_b, kn_b, vc, beta_c, gcc, gtail, gcr), kn, p in zip(heads, kns, ps):
        rhs = jnp.concatenate([vc * beta_c, kn * (beta_c * jnp.exp(gcc))], axis=1).astype(BF16)
        uws.append(dot(unpack(p), rhs).astype(BF16))
    auws = [dot(attn, uw) for attn, uw in zip(attns, uws)]
    q_effs = [qn * jnp.exp(h[4]) - auw[:, HEAD_DIM:] for qn, h, auw in zip(qns, heads, auws)]
    kes = [(kn * jnp.exp(h[5])).astype(BF16) for kn, h in zip(kns, heads)]

    outs = [[] for _ in heads]
    states = list(states)
    for i in range(nc):
        rows = slice(i * c, (i + 1) * c)
        ebs = [lax.dot_general(ke[rows], uw[rows], _TN, preferred_element_type=F32)
               for ke, uw in zip(kes, uws)]
        pos = [dot(jnp.concatenate([eb[:, HEAD_DIM:], q_eff[rows]], axis=0).astype(BF16), s.astype(BF16))
               for eb, q_eff, s in zip(ebs, q_effs, states)]
        for n, (po, eb, auw, h) in enumerate(zip(pos, ebs, auws, heads)):
            outs[n].append(po[HEAD_DIM:] + auw[rows, :HEAD_DIM])
            dec = jnp.exp(h[6][:, i * c + c - 1:i * c + c])
            states[n] = dec * states[n] - po[:HEAD_DIM] + eb[:, :HEAD_DIM]
    return [jnp.concatenate(o, axis=0) for o in outs], states


def _gdn_body(q_ref, k_ref, v_ref, z_ref, g_ref, gt_ref, al_ref, dt_ref, alr_ref, dtr_ref, nw_ref, o_ref,
              s_ref, grow_ref, *, heads_per_step, n_heads):
    r = GDN_ROWS
    first = pl.program_id(2) == 0

    @pl.when(first)
    def _():
        s_ref[...] = jnp.zeros(s_ref.shape, F32)

    graw = g_ref[...]
    beta_all = jax.nn.sigmoid(graw)
    g_all = -jnp.exp(al_ref[...]) * _softplus(graw + dt_ref[...])
    gc_all = _segmented_cumsum(g_all, 0)
    gtail_all = _segmented_cumsum(g_all, 0, reverse=True) - g_all
    gtraw = gt_ref[...]
    reps = r // N_GATE_COLS
    alr = jnp.concatenate([alr_ref[...]] * reps, axis=1)
    dtr = jnp.concatenate([dtr_ref[...]] * reps, axis=1)
    grow_ref[...] = _segmented_cumsum(-jnp.exp(alr) * _softplus(gtraw + dtr), 1)

    lane = lax.broadcasted_iota(jnp.int32, (r, N_GATE_COLS), 1)
    nw = nw_ref[...]
    heads = []
    for hh in range(heads_per_step):
        head = pl.program_id(1) * heads_per_step + hh
        cols = slice(hh * HEAD_DIM, (hh + 1) * HEAD_DIM)
        beta_c = jnp.sum(jnp.where(lane == head, beta_all, 0.0), axis=1, keepdims=True)
        gcc = jnp.sum(jnp.where(lane == head + n_heads, gc_all, 0.0), axis=1, keepdims=True)
        gtail = jnp.sum(jnp.where(lane == head + n_heads, gtail_all, 0.0), axis=1, keepdims=True)
        gcr = grow_ref[pl.ds(head + n_heads, 1), :]
        heads.append((q_ref[:, cols], k_ref[:, cols], v_ref[:, cols].astype(F32), beta_c, gcc, gtail, gcr))
    outs, states = _delta_rule_heads(heads, [s_ref[hh] for hh in range(heads_per_step)])
    for hh in range(heads_per_step):
        cols = slice(hh * HEAD_DIM, (hh + 1) * HEAD_DIM)
        s_ref[hh] = states[hh]
        o_ref[:, cols] = (_rms(outs[hh], nw) * _silu(z_ref[:, cols].astype(F32))).astype(BF16)


def _gdn(proj, gates, gates_t, al, dt, alr, dtr, norm_w, *, batch, seq, d_lin, heads_per_step):
    t = proj.shape[0]
    n_heads = d_lin // HEAD_DIM
    w = heads_per_step * HEAD_DIM
    nblk = d_lin // w
    r = GDN_ROWS
    steps = seq // r
    row_map = lambda b, g, i: b * steps + i
    ng = gates_t.shape[0]

    def col_spec(offset_blocks):
        return pl.BlockSpec((r, w), lambda b, g, i: (row_map(b, g, i), offset_blocks + g))

    small = lambda shape: pl.BlockSpec(shape, lambda b, g, i: (0, 0))
    body = functools.partial(_gdn_body, heads_per_step=heads_per_step, n_heads=n_heads)
    return pl.pallas_call(
        body,
        grid=(batch, nblk, steps),
        in_specs=[
            col_spec(0), col_spec(nblk), col_spec(2 * nblk), col_spec(3 * nblk),
            pl.BlockSpec((r, N_GATE_COLS), lambda b, g, i: (row_map(b, g, i), 0)),
            pl.BlockSpec((ng, r), lambda b, g, i: (0, row_map(b, g, i))),
            small((1, N_GATE_COLS)), small((1, N_GATE_COLS)),
            small((ng, N_GATE_COLS)), small((ng, N_GATE_COLS)),
            small((1, HEAD_DIM)),
        ],
        out_specs=pl.BlockSpec((r, w), lambda b, g, i: (row_map(b, g, i), g)),
        out_shape=jax.ShapeDtypeStruct((t, d_lin), BF16),
        scratch_shapes=[
            pltpu.VMEM((heads_per_step, HEAD_DIM, HEAD_DIM), F32),
            pltpu.VMEM((ng, r), F32),
        ],
        compiler_params=_params(("arbitrary", "arbitrary", "arbitrary")),
        name="gdn",
    )(proj, proj, proj, proj, gates, gates_t, al, dt, alr, dtr, norm_w)


def _mix_body(p_ref, ga_ref, gb_ref, ya_ref, x_ref, pw_ref, ps_ref, wo_ref, o_ref,
              pad_ref, mixed_ref, *, tiles_per_seq):
    tm = x_ref.shape[0]
    gdim = pw_ref.shape[1]
    seq_tile = pl.program_id(0) % tiles_per_seq
    nslab = pad_ref.shape[0]
    per_group = gdim // LANES

    @pl.when(seq_tile == 0)
    def _():
        pad_ref[:, 0:POOL_HALO, :] = jnp.zeros((nslab, POOL_HALO, LANES), F32)

    pos = seq_tile * tm + lax.broadcasted_iota(jnp.int32, (tm, 1), 0)
    for g, win in enumerate(POOL_WINDOWS):
        cols = slice(g * gdim, (g + 1) * gdim)
        cnt = jnp.minimum(pos + 1, win).astype(F32)
        pooled = []
        for c in range(g * per_group, (g + 1) * per_group):
            u = p_ref[:, c * LANES:(c + 1) * LANES].astype(F32)
            pad_ref[c, POOL_HALO:POOL_HALO + tm, :] = u
            acc = u
            for sft in range(1, win):
                acc = acc + pad_ref[c, pl.ds(POOL_HALO - sft, tm), :]
            pad_ref[c, 0:POOL_HALO, :] = pad_ref[c, tm:tm + POOL_HALO, :]
            pooled.append((acc / cnt - u).astype(BF16))
        pooled = jnp.concatenate(pooled, axis=1) if per_group > 1 else pooled[0]
        yb = jnp.dot(pooled, pw_ref[g], preferred_element_type=F32) * ps_ref[:, cols]
        mixed = (jax.nn.sigmoid(ga_ref[:, cols].astype(F32)) * ya_ref[:, cols].astype(F32)
                 + jax.nn.sigmoid(gb_ref[:, cols].astype(F32)) * yb)
        mixed_ref[:, cols] = mixed.astype(BF16)
    o_ref[...] = x_ref[...] + jnp.dot(mixed_ref[...], wo_ref[...], preferred_element_type=F32)


def _mix(proj, ya, x2, pool_w, pool_scale, w_out, *, seq, tm):
    t, d = x2.shape
    ngroups, gdim, _ = pool_w.shape
    assert gdim % LANES == 0
    first_blk = proj.shape[1] // d - 3
    row = lambda i: (i, 0)
    const1 = pl.Buffered(1)
    return pl.pallas_call(
        functools.partial(_mix_body, tiles_per_seq=seq // tm),
        grid=(t // tm,),
        in_specs=[
            pl.BlockSpec((tm, d), lambda i: (i, first_blk)),
            pl.BlockSpec((tm, d), lambda i: (i, first_blk + 1)),
            pl.BlockSpec((tm, d), lambda i: (i, first_blk + 2)),
            pl.BlockSpec((tm, d), row),
            pl.BlockSpec((tm, d), row),
            pl.BlockSpec((ngroups, gdim, gdim), lambda i: (0, 0, 0), pipeline_mode=const1),
            pl.BlockSpec((1, d), lambda i: (0, 0), pipeline_mode=const1),
            pl.BlockSpec((d, d), lambda i: (0, 0), pipeline_mode=const1),
        ],
        out_specs=pl.BlockSpec((tm, d), row),
        out_shape=jax.ShapeDtypeStruct((t, d), F32),
        scratch_shapes=[pltpu.VMEM((d // LANES, tm + POOL_HALO, LANES), F32), pltpu.VMEM((tm, d), BF16)],
        compiler_params=_params(("arbitrary",)),
        name="mix",
    )(proj, proj, proj, ya, x2, pool_w, pool_scale, w_out)


def _ffn_body(x_ref, nw_ref, wg_ref, wu_ref, cw_ref, cb_ref, wd_ref, o_ref,
              h_ref, pad_ref, carry_ref, *, tiles_per_seq):
    tm = x_ref.shape[0]
    j = pl.program_id(1)
    seq_start = pl.program_id(0) % tiles_per_seq == 0
    nslab = pad_ref.shape[0]

    @pl.when(j == 0)
    def _():
        x = x_ref[...]
        h_ref[...] = _rms(x, nw_ref[...]).astype(BF16)
        o_ref[...] = x

    h = h_ref[...]
    gate = jnp.dot(h, wg_ref[...], preferred_element_type=F32)
    up = jnp.dot(h, wu_ref[...], preferred_element_type=F32)

    @pl.when(seq_start)
    def _():
        pad_ref[:, 0:SUBLANES, :] = jnp.zeros((nslab, SUBLANES, LANES), F32)

    @pl.when(jnp.logical_not(seq_start))
    def _():
        for c in range(nslab):
            pad_ref[c, 0:SUBLANES, :] = carry_ref[j, :, c * LANES:(c + 1) * LANES]

    carry_ref[j] = gate[tm - SUBLANES:]
    base = SUBLANES - (CONV_FFN_TAPS - 1)
    acts = []
    for c in range(nslab):
        cols = slice(c * LANES, (c + 1) * LANES)
        pad_ref[c, SUBLANES:SUBLANES + tm, :] = gate[:, cols]
        conv = cb_ref[:, cols] + cw_ref[0:1, cols] * pad_ref[c, pl.ds(base, tm), :]
        for k in range(1, CONV_FFN_TAPS):
            conv = conv + cw_ref[k:k + 1, cols] * pad_ref[c, pl.ds(base + k, tm), :]
        acts.append((0.5 * conv * (1.0 + lax.erf(conv * (2.0 ** -0.5))) * up[:, cols]).astype(BF16))
    act = jnp.concatenate(acts, axis=1)
    o_ref[...] += jnp.dot(act, wd_ref[...], preferred_element_type=F32)


def _ffn(x2, norm_w, w_up, conv_w, conv_b, w_down, *, seq, tm, tf):
    t, d = x2.shape
    dff = w_down.shape[0]
    nf = dff // tf
    return pl.pallas_call(
        functools.partial(_ffn_body, tiles_per_seq=seq // tm),
        grid=(t // tm, nf),
        in_specs=[
            pl.BlockSpec((tm, d), lambda i, j: (i, 0)),
            pl.BlockSpec((1, d), lambda i, j: (0, 0)),
            pl.BlockSpec((d, tf), lambda i, j: (0, j)),
            pl.BlockSpec((d, tf), lambda i, j: (0, nf + j)),
            pl.BlockSpec((CONV_FFN_TAPS, tf), lambda i, j: (0, j)),
            pl.BlockSpec((1, tf), lambda i, j: (0, j)),
            pl.BlockSpec((tf, d), lambda i, j: (j, 0)),
        ],
        out_specs=pl.BlockSpec((tm, d), lambda i, j: (i, 0)),
        out_shape=jax.ShapeDtypeStruct((t, d), F32),
        scratch_shapes=[
            pltpu.VMEM((tm, d), BF16),
            pltpu.VMEM((tf // LANES, tm + SUBLANES, LANES), F32),
            pltpu.VMEM((nf, SUBLANES, tf), F32),
        ],
        compiler_params=_params(("arbitrary", "arbitrary")),
        name="ffn",
    )(x2, norm_w, w_up, w_up, conv_w, conv_b, w_down)


def _final_norm_body(x_ref, w_ref, o_ref):
    o_ref[...] = _rms(x_ref[...], w_ref[...])


def _final_norm(x2, w, *, tm):
    t, d = x2.shape
    return pl.pallas_call(
        _final_norm_body,
        grid=(t // tm,),
        in_specs=[pl.BlockSpec((tm, d), lambda i: (i, 0)), pl.BlockSpec((1, d), lambda i: (0, 0))],
        out_specs=pl.BlockSpec((tm, d), lambda i: (i, 0)),
        out_shape=jax.ShapeDtypeStruct((t, d), F32),
        compiler_params=_params(("arbitrary",)),
        name="final_norm",
    )(x2, w)


def _tile_plan(seq, d, dff):
    return dict(
        inproj_tm=_pick_tile(seq, 1024), inproj_tn=_pick_tile(d, 1024),
        mix_tm=_pick_tile(seq, 512),
        ffn_tm=_pick_tile(seq, 512), ffn_tf=_pick_tile(dff, 512),
        norm_tm=_pick_tile(seq, 512),
        heads_per_step=min(8, d // HEAD_DIM),
    )


def kernel(x, norm_mix_w, w_in, conv_qkv_w, a_log, dt_bias, gdn_norm_w, pool_w, pool_scale, w_out,
           norm_ffn_w, w_up, conv_ffn_w, conv_ffn_b, w_down, norm_final_w):
    bsz, seq, d = x.shape
    depth = w_in.shape[0]
    n_heads = d // HEAD_DIM
    dff = w_down.shape[1]
    assert seq % GDN_ROWS == 0 and d % HEAD_DIM == 0 and 2 * n_heads <= N_GATE_COLS
    assert w_in.shape[2] == 7 * d + 2 * n_heads
    plan = _tile_plan(seq, d, dff)
    x2 = x.reshape(bsz * seq, d)
    gate_lo, gate_hi = 4 * d, 4 * d + 2 * n_heads
    lane_pad = N_GATE_COLS - 2 * n_heads

    for l in range(depth):
        wl = w_in[l]
        w_main = jnp.concatenate([wl[:, :gate_lo], wl[:, gate_hi:]], axis=1).astype(BF16)
        w_gate = jnp.pad(wl[:, gate_lo:gate_hi].astype(BF16), ((0, 0), (0, lane_pad)))
        al = jnp.pad(a_log[l], (n_heads, lane_pad)).reshape(1, N_GATE_COLS)
        dt = jnp.pad(dt_bias[l], (n_heads, lane_pad)).reshape(1, N_GATE_COLS)
        alr = jnp.broadcast_to(jnp.pad(a_log[l], (n_heads, 0))[:, None], (2 * n_heads, N_GATE_COLS))
        dtr = jnp.broadcast_to(jnp.pad(dt_bias[l], (n_heads, 0))[:, None], (2 * n_heads, N_GATE_COLS))

        proj, gates, gates_t = _inproj(x2, norm_mix_w[l].reshape(1, d), w_main, w_gate, conv_qkv_w[l],
                                       n_gate_rows=2 * n_heads, seq=seq, tm=plan["inproj_tm"],
                                       tn=plan["inproj_tn"])
        ya = _gdn(proj, gates, gates_t, al, dt, alr, dtr, gdn_norm_w[l].reshape(1, HEAD_DIM),
                  batch=bsz, seq=seq, d_lin=d, heads_per_step=plan["heads_per_step"])
        x2 = _mix(proj, ya, x2, pool_w[l].astype(BF16), pool_scale[l].reshape(1, d), w_out[l].astype(BF16),
                  seq=seq, tm=plan["mix_tm"])
        x2 = _ffn(x2, norm_ffn_w[l].reshape(1, d), w_up[l].astype(BF16), conv_ffn_w[l],
                  conv_ffn_b[l].reshape(1, dff), w_down[l].astype(BF16),
                  seq=seq, tm=plan["ffn_tm"], tf=plan["ffn_tf"])
    y = _final_norm(x2, norm_final_w.reshape(1, d), tm=plan["norm_tm"])
    return y.reshape(bsz, seq, d)
```

```python
import functools

import jax
import jax.numpy as jnp
from jax import lax
from jax.experimental import pallas as pl
from jax.experimental.pallas import tpu as pltpu

F32 = jnp.float32
BF16 = jnp.bfloat16

HEAD_DIM = 128
LANES = 128
CHUNK = 64
GDN_ROWS = 256
CHUNKS_PER_STEP = GDN_ROWS // CHUNK
CONV_QKV_TAPS = 4
CONV_FFN_TAPS = 3
FFN_ROW_BLOCKS = 2
INPROJ_ROW_BLOCKS = 4
POOL_WINDOWS = (2, 4, 8, 16)
POOL_HALO = 16
SUBLANES = 8
N_GATE_COLS = 128
EPS = 1e-6
MASKED_LOG_DECAY = -1e30
VMEM_LIMIT_BYTES = 60 * 1024 * 1024

_NT = (((1,), (1,)), ((), ()))
_TN = (((0,), (0,)), ((), ()))


def _params(semantics):
    return pltpu.CompilerParams(dimension_semantics=semantics, vmem_limit_bytes=VMEM_LIMIT_BYTES)


def _pick_tile(n, want):
    t = min(n, want)
    while n % t:
        t //= 2
    return t


def _rms(x, w):
    return x * lax.rsqrt(jnp.mean(x * x, axis=-1, keepdims=True) + EPS) * w


def _silu(x):
    return x * jax.nn.sigmoid(x)


def _softplus(x):
    return jnp.maximum(x, 0.0) + jnp.log(1.0 + jnp.exp(-jnp.abs(x)))


def _inproj_body(x_ref, nw_ref, w_ref, wg_ref, cw_ref, o_ref, g_ref, gt_ref,
                 h_ref, pad_ref, carry_ref, *, tiles_per_seq, q_tiles):
    i, j = pl.program_id(0), pl.program_id(1)
    tm, tn = o_ref.shape

    @pl.when(j == 0)
    def _():
        h = _rms(x_ref[...], nw_ref[...]).astype(BF16)
        h_ref[...] = h
        g = jnp.dot(h, wg_ref[...], preferred_element_type=F32)
        g_ref[...] = g
        gt_ref[...] = g.T[:gt_ref.shape[0]]

    rm = tm // INPROJ_ROW_BLOCKS

    def block_acc(r):
        return jnp.dot(h_ref[r * rm:(r + 1) * rm, :], w_ref[...], preferred_element_type=F32)

    @pl.when(j >= 3 * q_tiles)
    def _():
        for r in range(INPROJ_ROW_BLOCKS):
            o_ref[r * rm:(r + 1) * rm, :] = block_acc(r).astype(BF16)

    @pl.when(j < 3 * q_tiles)
    def _():
        seq_start = i % tiles_per_seq == 0
        heads = tn // HEAD_DIM

        @pl.when(seq_start)
        def _():
            pad_ref[:, 0:SUBLANES, :] = jnp.zeros((heads, SUBLANES, HEAD_DIM), F32)

        @pl.when(jnp.logical_not(seq_start))
        def _():
            for hh in range(heads):
                pad_ref[hh, 0:SUBLANES, :] = carry_ref[j, :, hh * HEAD_DIM:(hh + 1) * HEAD_DIM]

        is_v = j >= 2 * q_tiles
        scale = jnp.where(j < q_tiles, HEAD_DIM ** -0.5, 1.0)
        base = SUBLANES - (CONV_QKV_TAPS - 1)
        for r in range(INPROJ_ROW_BLOCKS):
            acc = block_acc(r)
            for hh in range(heads):
                pad_ref[hh, SUBLANES + r * rm:SUBLANES + (r + 1) * rm, :] = acc[:, hh * HEAD_DIM:(hh + 1) * HEAD_DIM]
            if r == INPROJ_ROW_BLOCKS - 1:
                carry_ref[j] = acc[rm - SUBLANES:]
        for r in range(INPROJ_ROW_BLOCKS):
            for hh in range(heads):
                cols = slice(hh * HEAD_DIM, (hh + 1) * HEAD_DIM)
                y = cw_ref[0:1, cols] * pad_ref[hh, pl.ds(base + r * rm, rm), :]
                for t in range(1, CONV_QKV_TAPS):
                    y = y + cw_ref[t:t + 1, cols] * pad_ref[hh, pl.ds(base + t + r * rm, rm), :]
                y = _silu(y)
                inv = lax.rsqrt(jnp.sum(y * y, axis=-1, keepdims=True) + EPS) * scale
                o_ref[r * rm:(r + 1) * rm, cols] = (y * jnp.where(is_v, 1.0, inv)).astype(BF16)


def _inproj(x2, norm_w, w_main, w_gate, conv_w, *, layer, n_gate_rows, seq, tm, tn):
    t, d = x2.shape
    nm = w_main.shape[2]
    ng = n_gate_rows
    assert d % tn == 0
    q_tiles = d // tn
    body = functools.partial(_inproj_body, tiles_per_seq=seq // tm, q_tiles=q_tiles)
    return pl.pallas_call(
        body,
        grid=(t // tm, nm // tn),
        in_specs=[
            pl.BlockSpec((tm, d), lambda i, j: (i, 0)),
            pl.BlockSpec((None, 1, d), lambda i, j: (layer, 0, 0)),
            pl.BlockSpec((None, d, tn), lambda i, j: (layer, 0, j)),
            pl.BlockSpec((None, d, N_GATE_COLS), lambda i, j: (layer, 0, 0)),
            pl.BlockSpec((None, CONV_QKV_TAPS, tn), lambda i, j: (layer, 0, jnp.minimum(j, 3 * q_tiles - 1))),
        ],
        out_specs=[
            pl.BlockSpec((tm, tn), lambda i, j: (i, j)),
            pl.BlockSpec((tm, N_GATE_COLS), lambda i, j: (i, 0)),
            pl.BlockSpec((ng, tm), lambda i, j: (0, i)),
        ],
        out_shape=[
            jax.ShapeDtypeStruct((t, nm), BF16),
            jax.ShapeDtypeStruct((t, N_GATE_COLS), F32),
            jax.ShapeDtypeStruct((ng, t), F32),
        ],
        scratch_shapes=[
            pltpu.VMEM((tm, d), BF16),
            pltpu.VMEM((tn // HEAD_DIM, tm + SUBLANES, HEAD_DIM), F32),
            pltpu.VMEM((3 * q_tiles, SUBLANES, tn), F32),
        ],
        compiler_params=_params(("arbitrary", "arbitrary")),
        name="inproj",
    )(x2, norm_w, w_main, w_gate, conv_w)


def _segmented_cumsum(x, axis, reverse=False):
    n = x.shape[axis]
    pos = lax.broadcasted_iota(jnp.int32, x.shape, axis) % CHUNK
    shift = 1
    while shift < CHUNK:
        if reverse:
            x = x + jnp.where(pos < CHUNK - shift, pltpu.roll(x, n - shift, axis), 0.0)
        else:
            x = x + jnp.where(pos >= shift, pltpu.roll(x, shift, axis), 0.0)
        shift *= 2
    return x


def _delta_rule_heads(heads, states):
    r, c, nc = GDN_ROWS, CHUNK, CHUNKS_PER_STEP
    row = lax.broadcasted_iota(jnp.int32, (r, r), 0)
    col = lax.broadcasted_iota(jnp.int32, (r, r), 1)
    same_chunk = (row // c) == (col // c)
    causal = same_chunk & (row >= col)
    strict = same_chunk & (row > col)
    eye_cat = (lax.broadcasted_iota(jnp.int32, (c, r), 0)
               == lax.broadcasted_iota(jnp.int32, (c, r), 1) % c).astype(F32)
    chunk_mask = same_chunk.astype(F32).astype(BF16)

    def pack(x_bd):
        out = x_bd[0:c]
        for i in range(1, nc):
            out = out + x_bd[i * c:(i + 1) * c]
        return out

    def unpack(x_cat):
        return jnp.concatenate([x_cat.astype(BF16)] * nc, axis=0) * chunk_mask

    def dot(a, b):
        return jnp.dot(a, b, preferred_element_type=F32)

    qns, kns, attns, n_bds = [], [], [], []
    for qn_b, kn_b, vc, beta_c, gcc, gtail, gcr in heads:
        qn = qn_b.astype(F32)
        kn = kn_b.astype(F32)
        diff = gcc - gcr
        decay = jnp.exp(jnp.where(causal, diff, MASKED_LOG_DECAY))
        decay_strict = jnp.exp(jnp.where(strict, diff, MASKED_LOG_DECAY))
        kq = jnp.concatenate([(kn * (-beta_c)).astype(BF16), qn_b], axis=0)
        gram = lax.dot_general(kq, kn_b, _NT, preferred_element_type=F32)
        qns.append(qn)
        kns.append(kn)
        attns.append((gram[r:] * decay).astype(BF16))
        n_bds.append(gram[:r] * decay_strict)

    n_cats = [pack(n_bd) for n_bd in n_bds]
    ps = [eye_cat + n_cat for n_cat in n_cats]
    w_cats = [dot(n_cat.astype(BF16), n_bd.astype(BF16)) for n_cat, n_bd in zip(n_cats, n_bds)]
    power = 2
    while power < c // 2:
        xs = [dot(jnp.concatenate([p, w_cat], axis=0).astype(BF16), unpack(w_cat))
              for p, w_cat in zip(ps, w_cats)]
        ps = [p + x[:c] for p, x in zip(ps, xs)]
        w_cats = [x[c:] for x in xs]
        power *= 2
    ps = [p + dot(p.astype(BF16), unpack(w_cat)) for p, w_cat in zip(ps, w_cats)]

    uws = []
    for (qn_b, kn_b, vc, beta_c, gcc, gtail, gcr), kn, p in zip(heads, kns, ps):
        rhs = jnp.concatenate([vc * beta_c, kn * (beta_c * jnp.exp(gcc))], axis=1).astype(BF16)
        uws.append(dot(unpack(p), rhs).astype(BF16))
    auws = [dot(attn, uw) for attn, uw in zip(attns, uws)]
    q_effs = [qn * jnp.exp(h[4]) - auw[:, HEAD_DIM:] for qn, h, auw in zip(qns, heads, auws)]
    kes = [(kn * jnp.exp(h[5])).astype(BF16) for kn, h in zip(kns, heads)]

    outs = [[] for _ in heads]
    states = list(states)
    for i in range(nc):
        rows = slice(i * c, (i + 1) * c)
        ebs = [lax.dot_general(ke[rows], uw[rows], _TN, preferred_element_type=F32)
               for ke, uw in zip(kes, uws)]
        pos = [dot(jnp.concatenate([eb[:, HEAD_DIM:], q_eff[rows]], axis=0).astype(BF16), s.astype(BF16))
               for eb, q_eff, s in zip(ebs, q_effs, states)]
        for n, (po, eb, auw, h) in enumerate(zip(pos, ebs, auws, heads)):
            outs[n].append(po[HEAD_DIM:] + auw[rows, :HEAD_DIM])
            dec = jnp.exp(h[6][:, i * c + c - 1:i * c + c])
            states[n] = dec * states[n] - po[:HEAD_DIM] + eb[:, :HEAD_DIM]
    return [jnp.concatenate(o, axis=0) for o in outs], states


def _gdn_body(q_ref, k_ref, v_ref, z_ref, g_ref, gt_ref, al_ref, dt_ref, alr_ref, dtr_ref, nw_ref, o_ref,
              s_ref, grow_ref, *, heads_per_step, n_heads):
    r = GDN_ROWS
    first = pl.program_id(2) == 0

    @pl.when(first)
    def _():
        s_ref[...] = jnp.zeros(s_ref.shape, F32)

    graw = g_ref[...]
    beta_all = jax.nn.sigmoid(graw)
    g_all = -jnp.exp(al_ref[...]) * _softplus(graw + dt_ref[...])
    gc_all = _segmented_cumsum(g_all, 0)
    gtail_all = _segmented_cumsum(g_all, 0, reverse=True) - g_all
    gtraw = gt_ref[...]
    reps = r // N_GATE_COLS
    alr = jnp.concatenate([alr_ref[...]] * reps, axis=1)
    dtr = jnp.concatenate([dtr_ref[...]] * reps, axis=1)
    grow_ref[...] = _segmented_cumsum(-jnp.exp(alr) * _softplus(gtraw + dtr), 1)

    lane = lax.broadcasted_iota(jnp.int32, (r, N_GATE_COLS), 1)
    nw = nw_ref[...]
    heads = []
    for hh in range(heads_per_step):
        head = pl.program_id(1) * heads_per_step + hh
        cols = slice(hh * HEAD_DIM, (hh + 1) * HEAD_DIM)
        beta_c = jnp.sum(jnp.where(lane == head, beta_all, 0.0), axis=1, keepdims=True)
        gcc = jnp.sum(jnp.where(lane == head + n_heads, gc_all, 0.0), axis=1, keepdims=True)
        gtail = jnp.sum(jnp.where(lane == head + n_heads, gtail_all, 0.0), axis=1, keepdims=True)
        gcr = grow_ref[pl.ds(head + n_heads, 1), :]
        heads.append((q_ref[:, cols], k_ref[:, cols], v_ref[:, cols].astype(F32), beta_c, gcc, gtail, gcr))
    outs, states = _delta_rule_heads(heads, [s_ref[hh] for hh in range(heads_per_step)])
    for hh in range(heads_per_step):
        cols = slice(hh * HEAD_DIM, (hh + 1) * HEAD_DIM)
        s_ref[hh] = states[hh]
        o_ref[:, cols] = (_rms(outs[hh], nw) * _silu(z_ref[:, cols].astype(F32))).astype(BF16)


def _gdn(proj, gates, gates_t, al, dt, alr, dtr, norm_w, *, layer, batch, seq, d_lin, heads_per_step):
    t = proj.shape[0]
    n_heads = d_lin // HEAD_DIM
    w = heads_per_step * HEAD_DIM
    nblk = d_lin // w
    r = GDN_ROWS
    steps = seq // r
    row_map = lambda b, g, i: b * steps + i
    ng = gates_t.shape[0]

    def col_spec(offset_blocks):
        return pl.BlockSpec((r, w), lambda b, g, i: (row_map(b, g, i), offset_blocks + g))

    small = lambda shape: pl.BlockSpec((None,) + shape, lambda b, g, i: (layer, 0, 0))
    body = functools.partial(_gdn_body, heads_per_step=heads_per_step, n_heads=n_heads)
    return pl.pallas_call(
        body,
        grid=(batch, nblk, steps),
        in_specs=[
            col_spec(0), col_spec(nblk), col_spec(2 * nblk), col_spec(3 * nblk),
            pl.BlockSpec((r, N_GATE_COLS), lambda b, g, i: (row_map(b, g, i), 0)),
            pl.BlockSpec((ng, r), lambda b, g, i: (0, row_map(b, g, i))),
            small((1, N_GATE_COLS)), small((1, N_GATE_COLS)),
            small((ng, N_GATE_COLS)), small((ng, N_GATE_COLS)),
            small((1, HEAD_DIM)),
        ],
        out_specs=pl.BlockSpec((r, w), lambda b, g, i: (row_map(b, g, i), g)),
        out_shape=jax.ShapeDtypeStruct((t, d_lin), BF16),
        scratch_shapes=[
            pltpu.VMEM((heads_per_step, HEAD_DIM, HEAD_DIM), F32),
            pltpu.VMEM((ng, r), F32),
        ],
        compiler_params=_params(("arbitrary", "arbitrary", "arbitrary")),
        name="gdn",
    )(proj, proj, proj, proj, gates, gates_t, al, dt, alr, dtr, norm_w)


def _mix_body(p_ref, ga_ref, gb_ref, ya_ref, x_ref, pw_ref, ps_ref, wo_ref, o_ref,
              pad_ref, mixed_ref, *, tiles_per_seq):
    tm = x_ref.shape[0]
    gdim = pw_ref.shape[1]
    seq_tile = pl.program_id(0) % tiles_per_seq
    nslab = pad_ref.shape[0]
    per_group = gdim // LANES

    @pl.when(seq_tile == 0)
    def _():
        pad_ref[:, 0:POOL_HALO, :] = jnp.zeros((nslab, POOL_HALO, LANES), F32)

    pos = seq_tile * tm + lax.broadcasted_iota(jnp.int32, (tm, 1), 0)
    for g, win in enumerate(POOL_WINDOWS):
        cols = slice(g * gdim, (g + 1) * gdim)
        cnt = jnp.minimum(pos + 1, win).astype(F32)
        pooled = []
        for c in range(g * per_group, (g + 1) * per_group):
            u = p_ref[:, c * LANES:(c + 1) * LANES].astype(F32)
            pad_ref[c, POOL_HALO:POOL_HALO + tm, :] = u
            acc = u
            for sft in range(1, win):
                acc = acc + pad_ref[c, pl.ds(POOL_HALO - sft, tm), :]
            pad_ref[c, 0:POOL_HALO, :] = pad_ref[c, tm:tm + POOL_HALO, :]
            pooled.append((acc / cnt - u).astype(BF16))
        pooled = jnp.concatenate(pooled, axis=1) if per_group > 1 else pooled[0]
        yb = jnp.dot(pooled, pw_ref[g], preferred_element_type=F32) * ps_ref[:, cols]
        mixed = (jax.nn.sigmoid(ga_ref[:, cols].astype(F32)) * ya_ref[:, cols].astype(F32)
                 + jax.nn.sigmoid(gb_ref[:, cols].astype(F32)) * yb)
        mixed_ref[:, cols] = mixed.astype(BF16)
    o_ref[...] = x_ref[...] + jnp.dot(mixed_ref[...], wo_ref[...], preferred_element_type=F32)


def _mix(proj, ya, x2, pool_w, pool_scale, w_out, *, layer, seq, tm):
    t, d = x2.shape
    _, ngroups, gdim, _ = pool_w.shape
    assert gdim % LANES == 0
    first_blk = proj.shape[1] // d - 3
    row = lambda i: (i, 0)
    const1 = pl.Buffered(1)
    return pl.pallas_call(
        functools.partial(_mix_body, tiles_per_seq=seq // tm),
        grid=(t // tm,),
        in_specs=[
            pl.BlockSpec((tm, d), lambda i: (i, first_blk)),
            pl.BlockSpec((tm, d), lambda i: (i, first_blk + 1)),
            pl.BlockSpec((tm, d), lambda i: (i, first_blk + 2)),
            pl.BlockSpec((tm, d), row),
            pl.BlockSpec((tm, d), row),
            pl.BlockSpec((None, ngroups, gdim, gdim), lambda i: (layer, 0, 0, 0), pipeline_mode=const1),
            pl.BlockSpec((None, 1, d), lambda i: (layer, 0, 0), pipeline_mode=const1),
            pl.BlockSpec((None, d, d), lambda i: (layer, 0, 0), pipeline_mode=const1),
        ],
        out_specs=pl.BlockSpec((tm, d), row),
        out_shape=jax.ShapeDtypeStruct((t, d), F32),
        scratch_shapes=[pltpu.VMEM((d // LANES, tm + POOL_HALO, LANES), F32), pltpu.VMEM((tm, d), BF16)],
        compiler_params=_params(("arbitrary",)),
        name="mix",
    )(proj, proj, proj, ya, x2, pool_w, pool_scale, w_out)


def _ffn_body(x_ref, nw_ref, wg_ref, wu_ref, cw_ref, cb_ref, wd_ref, fw_ref, o_ref,
              h_ref, pad_ref, carry_ref, *, tiles_per_seq, final_norm):
    tm = x_ref.shape[0]
    j = pl.program_id(1)
    seq_start = pl.program_id(0) % tiles_per_seq == 0
    nslab = pad_ref.shape[0]

    @pl.when(j == 0)
    def _():
        x = x_ref[...]
        h_ref[...] = _rms(x, nw_ref[...]).astype(BF16)
        o_ref[...] = x

    @pl.when(seq_start)
    def _():
        pad_ref[:, 0:SUBLANES, :] = jnp.zeros((nslab, SUBLANES, LANES), F32)

    @pl.when(jnp.logical_not(seq_start))
    def _():
        for c in range(nslab):
            pad_ref[c, 0:SUBLANES, :] = carry_ref[j, :, c * LANES:(c + 1) * LANES]

    rm = tm // FFN_ROW_BLOCKS
    base = SUBLANES - (CONV_FFN_TAPS - 1)
    ups = []
    for r in range(FFN_ROW_BLOCKS):
        h = h_ref[r * rm:(r + 1) * rm, :]
        gate = jnp.dot(h, wg_ref[...], preferred_element_type=F32)
        ups.append(jnp.dot(h, wu_ref[...], preferred_element_type=F32))
        for c in range(nslab):
            pad_ref[c, SUBLANES + r * rm:SUBLANES + (r + 1) * rm, :] = gate[:, c * LANES:(c + 1) * LANES]
        if r == FFN_ROW_BLOCKS - 1:
            carry_ref[j] = gate[rm - SUBLANES:]
    for r in range(FFN_ROW_BLOCKS):
        acts = []
        for c in range(nslab):
            cols = slice(c * LANES, (c + 1) * LANES)
            conv = cb_ref[:, cols] + cw_ref[0:1, cols] * pad_ref[c, pl.ds(base + r * rm, rm), :]
            for k in range(1, CONV_FFN_TAPS):
                conv = conv + cw_ref[k:k + 1, cols] * pad_ref[c, pl.ds(base + k + r * rm, rm), :]
            acts.append((0.5 * conv * (1.0 + lax.erf(conv * (2.0 ** -0.5))) * ups[r][:, cols]).astype(BF16))
        act = jnp.concatenate(acts, axis=1)
        o_ref[r * rm:(r + 1) * rm, :] += jnp.dot(act, wd_ref[...], preferred_element_type=F32)

    if final_norm:
        @pl.when(j == pl.num_programs(1) - 1)
        def _():
            o_ref[...] = _rms(o_ref[...], fw_ref[...])


def _ffn(x2, norm_w, w_up, conv_w, conv_b, w_down, final_w, *, layer, final_norm, seq, tm, tf):
    t, d = x2.shape
    dff = w_down.shape[1]
    nf = dff // tf
    return pl.pallas_call(
        functools.partial(_ffn_body, tiles_per_seq=seq // tm, final_norm=final_norm),
        grid=(t // tm, nf),
        in_specs=[
            pl.BlockSpec((tm, d), lambda i, j: (i, 0)),
            pl.BlockSpec((None, 1, d), lambda i, j: (layer, 0, 0)),
            pl.BlockSpec((None, d, tf), lambda i, j: (layer, 0, j)),
            pl.BlockSpec((None, d, tf), lambda i, j: (layer, 0, nf + j)),
            pl.BlockSpec((None, CONV_FFN_TAPS, tf), lambda i, j: (layer, 0, j)),
            pl.BlockSpec((None, 1, tf), lambda i, j: (layer, 0, j)),
            pl.BlockSpec((None, tf, d), lambda i, j: (layer, j, 0)),
            pl.BlockSpec((1, d), lambda i, j: (0, 0)),
        ],
        out_specs=pl.BlockSpec((tm, d), lambda i, j: (i, 0)),
        out_shape=jax.ShapeDtypeStruct((t, d), F32),
        scratch_shapes=[
            pltpu.VMEM((tm, d), BF16),
            pltpu.VMEM((tf // LANES, tm + SUBLANES, LANES), F32),
            pltpu.VMEM((nf, SUBLANES, tf), F32),
        ],
        compiler_params=_params(("arbitrary", "arbitrary")),
        name="ffn",
    )(x2, norm_w, w_up, w_up, conv_w, conv_b, w_down, final_w)


def _tile_plan(seq, d, dff):
    return dict(
        inproj_tm=_pick_tile(seq, 1024), inproj_tn=_pick_tile(d, 1024),
        mix_tm=_pick_tile(seq, 512),
        ffn_tm=_pick_tile(seq, 512), ffn_tf=_pick_tile(dff, 512),
        heads_per_step=min(8, d // HEAD_DIM),
    )


def kernel(x, norm_mix_w, w_in, conv_qkv_w, a_log, dt_bias, gdn_norm_w, pool_w, pool_scale, w_out,
           norm_ffn_w, w_up, conv_ffn_w, conv_ffn_b, w_down, norm_final_w):
    bsz, seq, d = x.shape
    depth = w_in.shape[0]
    n_heads = d // HEAD_DIM
    dff = w_down.shape[1]
    assert seq % GDN_ROWS == 0 and d % HEAD_DIM == 0 and 2 * n_heads <= N_GATE_COLS
    assert w_in.shape[2] == 7 * d + 2 * n_heads
    plan = _tile_plan(seq, d, dff)
    x2 = x.reshape(bsz * seq, d)
    gate_lo, gate_hi = 4 * d, 4 * d + 2 * n_heads
    lane_pad = N_GATE_COLS - 2 * n_heads

    w_main = jnp.concatenate([w_in[:, :, :gate_lo], w_in[:, :, gate_hi:]], axis=2).astype(BF16)
    w_gate = jnp.pad(w_in[:, :, gate_lo:gate_hi].astype(BF16), ((0, 0), (0, 0), (0, lane_pad)))
    w_up_b, w_down_b, w_out_b, pool_w_b = (w.astype(BF16) for w in (w_up, w_down, w_out, pool_w))
    row3 = lambda a: a.reshape(depth, 1, a.shape[-1])
    al = row3(jnp.pad(a_log, ((0, 0), (n_heads, lane_pad))))
    dt = row3(jnp.pad(dt_bias, ((0, 0), (n_heads, lane_pad))))
    alr = jnp.broadcast_to(jnp.pad(a_log, ((0, 0), (n_heads, 0)))[:, :, None], (depth, 2 * n_heads, N_GATE_COLS))
    dtr = jnp.broadcast_to(jnp.pad(dt_bias, ((0, 0), (n_heads, 0)))[:, :, None], (depth, 2 * n_heads, N_GATE_COLS))
    norm_mix, norm_ffn, gdn_norm, pool_sc, ffn_b = (row3(a) for a in
                                                    (norm_mix_w, norm_ffn_w, gdn_norm_w, pool_scale, conv_ffn_b))

    for l in range(depth):
        proj, gates, gates_t = _inproj(x2, norm_mix, w_main, w_gate, conv_qkv_w, layer=l,
                                       n_gate_rows=2 * n_heads, seq=seq, tm=plan["inproj_tm"],
                                       tn=plan["inproj_tn"])
        ya = _gdn(proj, gates, gates_t, al, dt, alr, dtr, gdn_norm, layer=l,
                  batch=bsz, seq=seq, d_lin=d, heads_per_step=plan["heads_per_step"])
        x2 = _mix(proj, ya, x2, pool_w_b, pool_sc, w_out_b, layer=l, seq=seq, tm=plan["mix_tm"])
        x2 = _ffn(x2, norm_ffn, w_up_b, conv_ffn_w, ffn_b, w_down_b, norm_final_w.reshape(1, d), layer=l,
                  final_norm=(l == depth - 1), seq=seq, tm=plan["ffn_tm"], tf=plan["ffn_tf"])
    return x2.reshape(bsz, seq, d)
```

```python
import functools

import jax
import jax.numpy as jnp
from jax import lax
from jax.experimental import pallas as pl
from jax.experimental.pallas import tpu as pltpu

F32 = jnp.float32
BF16 = jnp.bfloat16

HEAD_DIM = 128
LANES = 128
CHUNK = 64
GDN_ROWS = 256
CHUNKS_PER_STEP = GDN_ROWS // CHUNK
CONV_QKV_TAPS = 4
CONV_FFN_TAPS = 3
FFN_ROW_BLOCKS = 2
INPROJ_ROW_BLOCKS = 4
POOL_WINDOWS = (2, 4, 8, 16)
POOL_HALO = 16
SUBLANES = 8
N_GATE_COLS = 128
EPS = 1e-6
MASKED_LOG_DECAY = -1e30
VMEM_LIMIT_BYTES = 60 * 1024 * 1024

_NT = (((1,), (1,)), ((), ()))
_TN = (((0,), (0,)), ((), ()))


def _params(semantics):
    return pltpu.CompilerParams(dimension_semantics=semantics, vmem_limit_bytes=VMEM_LIMIT_BYTES)


def _pick_tile(n, want):
    t = min(n, want)
    while n % t:
        t //= 2
    return t


def _rms(x, w):
    return x * lax.rsqrt(jnp.mean(x * x, axis=-1, keepdims=True) + EPS) * w


def _silu(x):
    return x * jax.nn.sigmoid(x)


def _softplus(x):
    return jnp.maximum(x, 0.0) + jnp.log(1.0 + jnp.exp(-jnp.abs(x)))


def _inproj_body(x_ref, nw_ref, w_ref, wg_ref, cw_ref, o_ref, g_ref, gt_ref,
                 h_ref, pad_ref, carry_ref, *, tiles_per_seq, q_tiles):
    i, j = pl.program_id(0), pl.program_id(1)
    tm, tn = o_ref.shape

    @pl.when(j == 0)
    def _():
        h = _rms(x_ref[...], nw_ref[...]).astype(BF16)
        h_ref[...] = h
        g = jnp.dot(h, wg_ref[...], preferred_element_type=F32)
        g_ref[...] = g
        gt_ref[...] = g.T[:gt_ref.shape[0]]

    rm = tm // INPROJ_ROW_BLOCKS

    def block_acc(r):
        return jnp.dot(h_ref[r * rm:(r + 1) * rm, :], w_ref[...], preferred_element_type=F32)

    @pl.when(j >= 3 * q_tiles)
    def _():
        for r in range(INPROJ_ROW_BLOCKS):
            o_ref[r * rm:(r + 1) * rm, :] = block_acc(r).astype(BF16)

    @pl.when(j < 3 * q_tiles)
    def _():
        seq_start = i % tiles_per_seq == 0
        heads = tn // HEAD_DIM

        @pl.when(seq_start)
        def _():
            pad_ref[:, 0:SUBLANES, :] = jnp.zeros((heads, SUBLANES, HEAD_DIM), F32)

        @pl.when(jnp.logical_not(seq_start))
        def _():
            for hh in range(heads):
                pad_ref[hh, 0:SUBLANES, :] = carry_ref[j, :, hh * HEAD_DIM:(hh + 1) * HEAD_DIM]

        is_v = j >= 2 * q_tiles
        scale = jnp.where(j < q_tiles, HEAD_DIM ** -0.5, 1.0)
        base = SUBLANES - (CONV_QKV_TAPS - 1)
        for r in range(INPROJ_ROW_BLOCKS):
            acc = block_acc(r)
            for hh in range(heads):
                pad_ref[hh, SUBLANES + r * rm:SUBLANES + (r + 1) * rm, :] = acc[:, hh * HEAD_DIM:(hh + 1) * HEAD_DIM]
            if r == INPROJ_ROW_BLOCKS - 1:
                carry_ref[j] = acc[rm - SUBLANES:]
        for r in range(INPROJ_ROW_BLOCKS):
            for hh in range(heads):
                cols = slice(hh * HEAD_DIM, (hh + 1) * HEAD_DIM)
                y = cw_ref[0:1, cols] * pad_ref[hh, pl.ds(base + r * rm, rm), :]
                for t in range(1, CONV_QKV_TAPS):
                    y = y + cw_ref[t:t + 1, cols] * pad_ref[hh, pl.ds(base + t + r * rm, rm), :]
                y = _silu(y)
                inv = lax.rsqrt(jnp.sum(y * y, axis=-1, keepdims=True) + EPS) * scale
                o_ref[r * rm:(r + 1) * rm, cols] = (y * jnp.where(is_v, 1.0, inv)).astype(BF16)


def _inproj(x2, norm_w, w_main, w_gate, conv_w, *, layer, n_gate_rows, seq, tm, tn):
    t, d = x2.shape
    nm = w_main.shape[1]
    ng = n_gate_rows
    assert d % tn == 0
    q_tiles = d // tn
    body = functools.partial(_inproj_body, tiles_per_seq=seq // tm, q_tiles=q_tiles)
    return pl.pallas_call(
        body,
        grid=(t // tm, nm // tn),
        in_specs=[
            pl.BlockSpec((tm, d), lambda i, j: (i, 0)),
            pl.BlockSpec((None, 1, d), lambda i, j: (layer, 0, 0)),
            pl.BlockSpec((d, tn), lambda i, j: (0, j)),
            pl.BlockSpec((None, d, N_GATE_COLS), lambda i, j: (layer, 0, 0)),
            pl.BlockSpec((None, CONV_QKV_TAPS, tn), lambda i, j: (layer, 0, jnp.minimum(j, 3 * q_tiles - 1))),
        ],
        out_specs=[
            pl.BlockSpec((tm, tn), lambda i, j: (i, j)),
            pl.BlockSpec((tm, N_GATE_COLS), lambda i, j: (i, 0)),
            pl.BlockSpec((ng, tm), lambda i, j: (0, i)),
        ],
        out_shape=[
            jax.ShapeDtypeStruct((t, nm), BF16),
            jax.ShapeDtypeStruct((t, N_GATE_COLS), F32),
            jax.ShapeDtypeStruct((ng, t), F32),
        ],
        scratch_shapes=[
            pltpu.VMEM((tm, d), BF16),
            pltpu.VMEM((tn // HEAD_DIM, tm + SUBLANES, HEAD_DIM), F32),
            pltpu.VMEM((3 * q_tiles, SUBLANES, tn), F32),
        ],
        compiler_params=_params(("arbitrary", "arbitrary")),
        name="inproj",
    )(x2, norm_w, w_main, w_gate, conv_w)


def _segmented_cumsum(x, axis, reverse=False):
    n = x.shape[axis]
    pos = lax.broadcasted_iota(jnp.int32, x.shape, axis) % CHUNK
    shift = 1
    while shift < CHUNK:
        if reverse:
            x = x + jnp.where(pos < CHUNK - shift, pltpu.roll(x, n - shift, axis), 0.0)
        else:
            x = x + jnp.where(pos >= shift, pltpu.roll(x, shift, axis), 0.0)
        shift *= 2
    return x


def _delta_rule_heads(heads, states):
    r, c, nc = GDN_ROWS, CHUNK, CHUNKS_PER_STEP
    row = lax.broadcasted_iota(jnp.int32, (r, r), 0)
    col = lax.broadcasted_iota(jnp.int32, (r, r), 1)
    same_chunk = (row // c) == (col // c)
    causal = same_chunk & (row >= col)
    strict = same_chunk & (row > col)
    eye_cat = (lax.broadcasted_iota(jnp.int32, (c, r), 0)
               == lax.broadcasted_iota(jnp.int32, (c, r), 1) % c).astype(F32)
    chunk_mask = same_chunk.astype(F32).astype(BF16)

    def pack(x_bd):
        out = x_bd[0:c]
        for i in range(1, nc):
            out = out + x_bd[i * c:(i + 1) * c]
        return out

    def unpack(x_cat):
        return jnp.concatenate([x_cat.astype(BF16)] * nc, axis=0) * chunk_mask

    def dot(a, b):
        return jnp.dot(a, b, preferred_element_type=F32)

    qns, kns, attns, n_bds = [], [], [], []
    for qn_b, kn_b, vc, beta_c, gcc, gtail, gcr in heads:
        qn = qn_b.astype(F32)
        kn = kn_b.astype(F32)
        diff = gcc - gcr
        decay = jnp.exp(jnp.where(causal, diff, MASKED_LOG_DECAY))
        decay_strict = jnp.exp(jnp.where(strict, diff, MASKED_LOG_DECAY))
        kq = jnp.concatenate([(kn * (-beta_c)).astype(BF16), qn_b], axis=0)
        gram = lax.dot_general(kq, kn_b, _NT, preferred_element_type=F32)
        qns.append(qn)
        kns.append(kn)
        attns.append((gram[r:] * decay).astype(BF16))
        n_bds.append(gram[:r] * decay_strict)

    n_cats = [pack(n_bd) for n_bd in n_bds]
    ps = [eye_cat + n_cat for n_cat in n_cats]
    w_cats = [dot(n_cat.astype(BF16), n_bd.astype(BF16)) for n_cat, n_bd in zip(n_cats, n_bds)]
    power = 2
    while power < c // 2:
        xs = [dot(jnp.concatenate([p, w_cat], axis=0).astype(BF16), unpack(w_cat))
              for p, w_cat in zip(ps, w_cats)]
        ps = [p + x[:c] for p, x in zip(ps, xs)]
        w_cats = [x[c:] for x in xs]
        power *= 2
    ps = [p + dot(p.astype(BF16), unpack(w_cat)) for p, w_cat in zip(ps, w_cats)]

    uws = []
    for (qn_b, kn_b, vc, beta_c, gcc, gtail, gcr), kn, p in zip(heads, kns, ps):
        rhs = jnp.concatenate([vc * beta_c, kn * (beta_c * jnp.exp(gcc))], axis=1).astype(BF16)
        uws.append(dot(unpack(p), rhs).astype(BF16))
    auws = [dot(attn, uw) for attn, uw in zip(attns, uws)]
    q_effs = [qn * jnp.exp(h[4]) - auw[:, HEAD_DIM:] for qn, h, auw in zip(qns, heads, auws)]
    kes = [(kn * jnp.exp(h[5])).astype(BF16) for kn, h in zip(kns, heads)]

    outs = [[] for _ in heads]
    states = list(states)
    for i in range(nc):
        rows = slice(i * c, (i + 1) * c)
        ebs = [lax.dot_general(ke[rows], uw[rows], _TN, preferred_element_type=F32)
               for ke, uw in zip(kes, uws)]
        pos = [dot(jnp.concatenate([eb[:, HEAD_DIM:], q_eff[rows]], axis=0).astype(BF16), s.astype(BF16))
               for eb, q_eff, s in zip(ebs, q_effs, states)]
        for n, (po, eb, auw, h) in enumerate(zip(pos, ebs, auws, heads)):
            outs[n].append(po[HEAD_DIM:] + auw[rows, :HEAD_DIM])
            dec = jnp.exp(h[6][:, i * c + c - 1:i * c + c])
            states[n] = dec * states[n] - po[:HEAD_DIM] + eb[:, :HEAD_DIM]
    return [jnp.concatenate(o, axis=0) for o in outs], states


def _gdn_body(q_ref, k_ref, v_ref, z_ref, g_ref, gt_ref, al_ref, dt_ref, alr_ref, dtr_ref, nw_ref, o_ref,
              s_ref, grow_ref, *, heads_per_step, n_heads):
    r = GDN_ROWS
    first = pl.program_id(2) == 0

    @pl.when(first)
    def _():
        s_ref[...] = jnp.zeros(s_ref.shape, F32)

    graw = g_ref[...]
    beta_all = jax.nn.sigmoid(graw)
    g_all = -jnp.exp(al_ref[...]) * _softplus(graw + dt_ref[...])
    gc_all = _segmented_cumsum(g_all, 0)
    gtail_all = _segmented_cumsum(g_all, 0, reverse=True) - g_all
    gtraw = gt_ref[...]
    reps = r // N_GATE_COLS
    alr = jnp.concatenate([alr_ref[...]] * reps, axis=1)
    dtr = jnp.concatenate([dtr_ref[...]] * reps, axis=1)
    grow_ref[...] = _segmented_cumsum(-jnp.exp(alr) * _softplus(gtraw + dtr), 1)

    lane = lax.broadcasted_iota(jnp.int32, (r, N_GATE_COLS), 1)
    nw = nw_ref[...]
    heads = []
    for hh in range(heads_per_step):
        head = pl.program_id(1) * heads_per_step + hh
        cols = slice(hh * HEAD_DIM, (hh + 1) * HEAD_DIM)
        beta_c = jnp.sum(jnp.where(lane == head, beta_all, 0.0), axis=1, keepdims=True)
        gcc = jnp.sum(jnp.where(lane == head + n_heads, gc_all, 0.0), axis=1, keepdims=True)
        gtail = jnp.sum(jnp.where(lane == head + n_heads, gtail_all, 0.0), axis=1, keepdims=True)
        gcr = grow_ref[pl.ds(head + n_heads, 1), :]
        heads.append((q_ref[:, cols], k_ref[:, cols], v_ref[:, cols].astype(F32), beta_c, gcc, gtail, gcr))
    outs, states = _delta_rule_heads(heads, [s_ref[hh] for hh in range(heads_per_step)])
    for hh in range(heads_per_step):
        cols = slice(hh * HEAD_DIM, (hh + 1) * HEAD_DIM)
        s_ref[hh] = states[hh]
        o_ref[:, cols] = (_rms(outs[hh], nw) * _silu(z_ref[:, cols].astype(F32))).astype(BF16)


def _gdn(proj, gates, gates_t, al, dt, alr, dtr, norm_w, *, layer, batch, seq, d_lin, heads_per_step):
    t = proj.shape[0]
    n_heads = d_lin // HEAD_DIM
    w = heads_per_step * HEAD_DIM
    nblk = d_lin // w
    r = GDN_ROWS
    steps = seq // r
    row_map = lambda b, g, i: b * steps + i
    ng = gates_t.shape[0]

    def col_spec(offset_blocks):
        return pl.BlockSpec((r, w), lambda b, g, i: (row_map(b, g, i), offset_blocks + g))

    small = lambda shape: pl.BlockSpec((None,) + shape, lambda b, g, i: (layer, 0, 0))
    body = functools.partial(_gdn_body, heads_per_step=heads_per_step, n_heads=n_heads)
    return pl.pallas_call(
        body,
        grid=(batch, nblk, steps),
        in_specs=[
            col_spec(0), col_spec(nblk), col_spec(2 * nblk), col_spec(3 * nblk),
            pl.BlockSpec((r, N_GATE_COLS), lambda b, g, i: (row_map(b, g, i), 0)),
            pl.BlockSpec((ng, r), lambda b, g, i: (0, row_map(b, g, i))),
            small((1, N_GATE_COLS)), small((1, N_GATE_COLS)),
            small((ng, N_GATE_COLS)), small((ng, N_GATE_COLS)),
            small((1, HEAD_DIM)),
        ],
        out_specs=pl.BlockSpec((r, w), lambda b, g, i: (row_map(b, g, i), g)),
        out_shape=jax.ShapeDtypeStruct((t, d_lin), BF16),
        scratch_shapes=[
            pltpu.VMEM((heads_per_step, HEAD_DIM, HEAD_DIM), F32),
            pltpu.VMEM((ng, r), F32),
        ],
        compiler_params=_params(("arbitrary", "arbitrary", "arbitrary")),
        name="gdn",
    )(proj, proj, proj, proj, gates, gates_t, al, dt, alr, dtr, norm_w)


def _mix_body(p_ref, ga_ref, gb_ref, ya_ref, x_ref, pw_ref, ps_ref, wo_ref, o_ref,
              pad_ref, mixed_ref, *, tiles_per_seq):
    tm = x_ref.shape[0]
    gdim = pw_ref.shape[1]
    seq_tile = pl.program_id(0) % tiles_per_seq
    nslab = pad_ref.shape[0]
    per_group = gdim // LANES

    @pl.when(seq_tile == 0)
    def _():
        pad_ref[:, 0:POOL_HALO, :] = jnp.zeros((nslab, POOL_HALO, LANES), F32)

    pos = seq_tile * tm + lax.broadcasted_iota(jnp.int32, (tm, 1), 0)
    for g, win in enumerate(POOL_WINDOWS):
        cols = slice(g * gdim, (g + 1) * gdim)
        cnt = jnp.minimum(pos + 1, win).astype(F32)
        pooled = []
        for c in range(g * per_group, (g + 1) * per_group):
            u = p_ref[:, c * LANES:(c + 1) * LANES].astype(F32)
            pad_ref[c, POOL_HALO:POOL_HALO + tm, :] = u
            acc = u
            for sft in range(1, win):
                acc = acc + pad_ref[c, pl.ds(POOL_HALO - sft, tm), :]
            pad_ref[c, 0:POOL_HALO, :] = pad_ref[c, tm:tm + POOL_HALO, :]
            pooled.append((acc / cnt - u).astype(BF16))
        pooled = jnp.concatenate(pooled, axis=1) if per_group > 1 else pooled[0]
        yb = jnp.dot(pooled, pw_ref[g], preferred_element_type=F32) * ps_ref[:, cols]
        mixed = (jax.nn.sigmoid(ga_ref[:, cols].astype(F32)) * ya_ref[:, cols].astype(F32)
                 + jax.nn.sigmoid(gb_ref[:, cols].astype(F32)) * yb)
        mixed_ref[:, cols] = mixed.astype(BF16)
    o_ref[...] = x_ref[...] + jnp.dot(mixed_ref[...], wo_ref[...], preferred_element_type=F32)


def _mix(proj, ya, x2, pool_w, pool_scale, w_out, *, layer, seq, tm):
    t, d = x2.shape
    _, ngroups, gdim, _ = pool_w.shape
    assert gdim % LANES == 0
    first_blk = proj.shape[1] // d - 3
    row = lambda i: (i, 0)
    const1 = pl.Buffered(1)
    return pl.pallas_call(
        functools.partial(_mix_body, tiles_per_seq=seq // tm),
        grid=(t // tm,),
        in_specs=[
            pl.BlockSpec((tm, d), lambda i: (i, first_blk)),
            pl.BlockSpec((tm, d), lambda i: (i, first_blk + 1)),
            pl.BlockSpec((tm, d), lambda i: (i, first_blk + 2)),
            pl.BlockSpec((tm, d), row),
            pl.BlockSpec((tm, d), row),
            pl.BlockSpec((None, ngroups, gdim, gdim), lambda i: (layer, 0, 0, 0), pipeline_mode=const1),
            pl.BlockSpec((None, 1, d), lambda i: (layer, 0, 0), pipeline_mode=const1),
            pl.BlockSpec((None, d, d), lambda i: (layer, 0, 0), pipeline_mode=const1),
        ],
        out_specs=pl.BlockSpec((tm, d), row),
        out_shape=jax.ShapeDtypeStruct((t, d), F32),
        scratch_shapes=[pltpu.VMEM((d // LANES, tm + POOL_HALO, LANES), F32), pltpu.VMEM((tm, d), BF16)],
        compiler_params=_params(("arbitrary",)),
        name="mix",
    )(proj, proj, proj, ya, x2, pool_w, pool_scale, w_out)


def _ffn_body(x_ref, nw_ref, wg_ref, wu_ref, cw_ref, cb_ref, wd_ref, fw_ref, o_ref,
              h_ref, pad_ref, carry_ref, *, tiles_per_seq, final_norm):
    tm = x_ref.shape[0]
    j = pl.program_id(1)
    seq_start = pl.program_id(0) % tiles_per_seq == 0
    nslab = pad_ref.shape[0]

    @pl.when(j == 0)
    def _():
        x = x_ref[...]
        h_ref[...] = _rms(x, nw_ref[...]).astype(BF16)
        o_ref[...] = x

    @pl.when(seq_start)
    def _():
        pad_ref[:, 0:SUBLANES, :] = jnp.zeros((nslab, SUBLANES, LANES), F32)

    @pl.when(jnp.logical_not(seq_start))
    def _():
        for c in range(nslab):
            pad_ref[c, 0:SUBLANES, :] = carry_ref[j, :, c * LANES:(c + 1) * LANES]

    rm = tm // FFN_ROW_BLOCKS
    base = SUBLANES - (CONV_FFN_TAPS - 1)
    ups = []
    for r in range(FFN_ROW_BLOCKS):
        h = h_ref[r * rm:(r + 1) * rm, :]
        gate = jnp.dot(h, wg_ref[...], preferred_element_type=F32)
        ups.append(jnp.dot(h, wu_ref[...], preferred_element_type=F32))
        for c in range(nslab):
            pad_ref[c, SUBLANES + r * rm:SUBLANES + (r + 1) * rm, :] = gate[:, c * LANES:(c + 1) * LANES]
        if r == FFN_ROW_BLOCKS - 1:
            carry_ref[j] = gate[rm - SUBLANES:]
    for r in range(FFN_ROW_BLOCKS):
        acts = []
        for c in range(nslab):
            cols = slice(c * LANES, (c + 1) * LANES)
            conv = cb_ref[:, cols] + cw_ref[0:1, cols] * pad_ref[c, pl.ds(base + r * rm, rm), :]
            for k in range(1, CONV_FFN_TAPS):
                conv = conv + cw_ref[k:k + 1, cols] * pad_ref[c, pl.ds(base + k + r * rm, rm), :]
            acts.append((0.5 * conv * (1.0 + lax.erf(conv * (2.0 ** -0.5))) * ups[r][:, cols]).astype(BF16))
        act = jnp.concatenate(acts, axis=1)
        o_ref[r * rm:(r + 1) * rm, :] += jnp.dot(act, wd_ref[...], preferred_element_type=F32)

    if final_norm:
        @pl.when(j == pl.num_programs(1) - 1)
        def _():
            o_ref[...] = _rms(o_ref[...], fw_ref[...])


def _ffn(x2, norm_w, w_up, conv_w, conv_b, w_down, final_w, *, layer, final_norm, seq, tm, tf):
    t, d = x2.shape
    dff = w_down.shape[1]
    nf = dff // tf
    return pl.pallas_call(
        functools.partial(_ffn_body, tiles_per_seq=seq // tm, final_norm=final_norm),
        grid=(t // tm, nf),
        in_specs=[
            pl.BlockSpec((tm, d), lambda i, j: (i, 0)),
            pl.BlockSpec((None, 1, d), lambda i, j: (layer, 0, 0)),
            pl.BlockSpec((None, d, tf), lambda i, j: (layer, 0, j)),
            pl.BlockSpec((None, d, tf), lambda i, j: (layer, 0, nf + j)),
            pl.BlockSpec((None, CONV_FFN_TAPS, tf), lambda i, j: (layer, 0, j)),
            pl.BlockSpec((None, 1, tf), lambda i, j: (layer, 0, j)),
            pl.BlockSpec((None, tf, d), lambda i, j: (layer, j, 0)),
            pl.BlockSpec((1, d), lambda i, j: (0, 0)),
        ],
        out_specs=pl.BlockSpec((tm, d), lambda i, j: (i, 0)),
        out_shape=jax.ShapeDtypeStruct((t, d), F32),
        scratch_shapes=[
            pltpu.VMEM((tm, d), BF16),
            pltpu.VMEM((tf // LANES, tm + SUBLANES, LANES), F32),
            pltpu.VMEM((nf, SUBLANES, tf), F32),
        ],
        compiler_params=_params(("arbitrary", "arbitrary")),
        name="ffn",
    )(x2, norm_w, w_up, w_up, conv_w, conv_b, w_down, final_w)


def _tile_plan(seq, d, dff):
    return dict(
        inproj_tm=_pick_tile(seq, 1024), inproj_tn=_pick_tile(d, 1024),
        mix_tm=_pick_tile(seq, 512),
        ffn_tm=_pick_tile(seq, 1024), ffn_tf=_pick_tile(dff, 512),
        heads_per_step=min(16, d // HEAD_DIM),
    )


def kernel(x, norm_mix_w, w_in, conv_qkv_w, a_log, dt_bias, gdn_norm_w, pool_w, pool_scale, w_out,
           norm_ffn_w, w_up, conv_ffn_w, conv_ffn_b, w_down, norm_final_w):
    bsz, seq, d = x.shape
    depth = w_in.shape[0]
    n_heads = d // HEAD_DIM
    dff = w_down.shape[1]
    assert seq % GDN_ROWS == 0 and d % HEAD_DIM == 0 and 2 * n_heads <= N_GATE_COLS
    assert w_in.shape[2] == 7 * d + 2 * n_heads
    plan = _tile_plan(seq, d, dff)
    x2 = x.reshape(bsz * seq, d)
    gate_lo, gate_hi = 4 * d, 4 * d + 2 * n_heads
    lane_pad = N_GATE_COLS - 2 * n_heads

    w_gate = jnp.pad(w_in[:, :, gate_lo:gate_hi].astype(BF16), ((0, 0), (0, 0), (0, lane_pad)))
    w_up_b, w_down_b, w_out_b, pool_w_b = (w.astype(BF16) for w in (w_up, w_down, w_out, pool_w))
    row3 = lambda a: a.reshape(depth, 1, a.shape[-1])
    al = row3(jnp.pad(a_log, ((0, 0), (n_heads, lane_pad))))
    dt = row3(jnp.pad(dt_bias, ((0, 0), (n_heads, lane_pad))))
    alr = jnp.broadcast_to(jnp.pad(a_log, ((0, 0), (n_heads, 0)))[:, :, None], (depth, 2 * n_heads, N_GATE_COLS))
    dtr = jnp.broadcast_to(jnp.pad(dt_bias, ((0, 0), (n_heads, 0)))[:, :, None], (depth, 2 * n_heads, N_GATE_COLS))
    norm_mix, norm_ffn, gdn_norm, pool_sc, ffn_b = (row3(a) for a in
                                                    (norm_mix_w, norm_ffn_w, gdn_norm_w, pool_scale, conv_ffn_b))

    for l in range(depth):
        w_main = jnp.concatenate([w_in[l, :, :gate_lo], w_in[l, :, gate_hi:]], axis=1).astype(BF16)
        proj, gates, gates_t = _inproj(x2, norm_mix, w_main, w_gate, conv_qkv_w, layer=l,
                                       n_gate_rows=2 * n_heads, seq=seq, tm=plan["inproj_tm"],
                                       tn=plan["inproj_tn"])
        ya = _gdn(proj, gates, gates_t, al, dt, alr, dtr, gdn_norm, layer=l,
                  batch=bsz, seq=seq, d_lin=d, heads_per_step=plan["heads_per_step"])
        x2 = _mix(proj, ya, x2, pool_w_b, pool_sc, w_out_b, layer=l, seq=seq, tm=plan["mix_tm"])
        x2 = _ffn(x2, norm_ffn, w_up_b, conv_ffn_w, ffn_b, w_down_b, norm_final_w.reshape(1, d), layer=l,
                  final_norm=(l == depth - 1), seq=seq, tm=plan["ffn_tm"], tf=plan["ffn_tf"])
    return x2.reshape(bsz, seq, d)
```

```python
import functools

import jax
import jax.numpy as jnp
from jax import lax
from jax.experimental import pallas as pl
from jax.experimental.pallas import tpu as pltpu

F32 = jnp.float32
BF16 = jnp.bfloat16

HEAD_DIM = 128
LANES = 128
CHUNK = 64
GDN_ROWS = 256
CHUNKS_PER_STEP = GDN_ROWS // CHUNK
CONV_QKV_TAPS = 4
CONV_FFN_TAPS = 3
FFN_ROW_BLOCKS = 2
INPROJ_ROW_BLOCKS = 4
POOL_WINDOWS = (2, 4, 8, 16)
POOL_HALO = 16
SUBLANES = 8
N_GATE_COLS = 128
EPS = 1e-6
MASKED_LOG_DECAY = -1e30
VMEM_LIMIT_BYTES = 60 * 1024 * 1024

_NT = (((1,), (1,)), ((), ()))
_TN = (((0,), (0,)), ((), ()))


def _params(semantics):
    return pltpu.CompilerParams(dimension_semantics=semantics, vmem_limit_bytes=VMEM_LIMIT_BYTES)


def _pick_tile(n, want):
    t = min(n, want)
    while n % t:
        t //= 2
    return t


def _rms(x, w):
    return x * lax.rsqrt(jnp.mean(x * x, axis=-1, keepdims=True) + EPS) * w


def _silu(x):
    return x * jax.nn.sigmoid(x)


def _softplus(x):
    return jnp.maximum(x, 0.0) + jnp.log(1.0 + jnp.exp(-jnp.abs(x)))


def _inproj_body(x_ref, nw_ref, w_ref, wg_ref, cw_ref, o_ref, g_ref, gt_ref,
                 h_ref, pad_ref, carry_ref, *, tiles_per_seq, q_tiles):
    i, j = pl.program_id(0), pl.program_id(1)
    tm, tn = o_ref.shape

    @pl.when(j == 0)
    def _():
        h = _rms(x_ref[...], nw_ref[...]).astype(BF16)
        h_ref[...] = h
        g = jnp.dot(h, wg_ref[...], preferred_element_type=F32)
        g_ref[...] = g
        gt_ref[...] = g.T[:gt_ref.shape[0]]

    rm = tm // INPROJ_ROW_BLOCKS

    def block_acc(r):
        return jnp.dot(h_ref[r * rm:(r + 1) * rm, :], w_ref[...], preferred_element_type=F32)

    @pl.when(j >= 3 * q_tiles)
    def _():
        for r in range(INPROJ_ROW_BLOCKS):
            o_ref[r * rm:(r + 1) * rm, :] = block_acc(r).astype(BF16)

    @pl.when(j < 3 * q_tiles)
    def _():
        seq_start = i % tiles_per_seq == 0
        heads = tn // HEAD_DIM

        @pl.when(seq_start)
        def _():
            pad_ref[:, 0:SUBLANES, :] = jnp.zeros((heads, SUBLANES, HEAD_DIM), F32)

        @pl.when(jnp.logical_not(seq_start))
        def _():
            for hh in range(heads):
                pad_ref[hh, 0:SUBLANES, :] = carry_ref[j, :, hh * HEAD_DIM:(hh + 1) * HEAD_DIM]

        is_v = j >= 2 * q_tiles
        scale = jnp.where(j < q_tiles, HEAD_DIM ** -0.5, 1.0)
        base = SUBLANES - (CONV_QKV_TAPS - 1)
        for r in range(INPROJ_ROW_BLOCKS):
            acc = block_acc(r)
            for hh in range(heads):
                pad_ref[hh, SUBLANES + r * rm:SUBLANES + (r + 1) * rm, :] = acc[:, hh * HEAD_DIM:(hh + 1) * HEAD_DIM]
            if r == INPROJ_ROW_BLOCKS - 1:
                carry_ref[j] = acc[rm - SUBLANES:]
        for r in range(INPROJ_ROW_BLOCKS):
            for hh in range(heads):
                cols = slice(hh * HEAD_DIM, (hh + 1) * HEAD_DIM)
                y = cw_ref[0:1, cols] * pad_ref[hh, pl.ds(base + r * rm, rm), :]
                for t in range(1, CONV_QKV_TAPS):
                    y = y + cw_ref[t:t + 1, cols] * pad_ref[hh, pl.ds(base + t + r * rm, rm), :]
                y = _silu(y)
                inv = lax.rsqrt(jnp.sum(y * y, axis=-1, keepdims=True) + EPS) * scale
                o_ref[r * rm:(r + 1) * rm, cols] = (y * jnp.where(is_v, 1.0, inv)).astype(BF16)


def _inproj(x2, norm_w, w_main, w_gate, conv_w, *, layer, n_gate_rows, seq, tm, tn):
    t, d = x2.shape
    nm = w_main.shape[1]
    ng = n_gate_rows
    assert d % tn == 0 and nm % tn == 0
    q_tiles = d // tn
    body = functools.partial(_inproj_body, tiles_per_seq=seq // tm, q_tiles=q_tiles)
    return pl.pallas_call(
        body,
        grid=(t // tm, nm // tn),
        in_specs=[
            pl.BlockSpec((tm, d), lambda i, j: (i, 0)),
            pl.BlockSpec((None, 1, d), lambda i, j: (layer, 0, 0)),
            pl.BlockSpec((d, tn), lambda i, j: (0, j)),
            pl.BlockSpec((d, N_GATE_COLS), lambda i, j: (0, 0)),
            pl.BlockSpec((None, CONV_QKV_TAPS, tn), lambda i, j: (layer, 0, jnp.minimum(j, 3 * q_tiles - 1))),
        ],
        out_specs=[
            pl.BlockSpec((tm, tn), lambda i, j: (i, j)),
            pl.BlockSpec((tm, N_GATE_COLS), lambda i, j: (i, 0)),
            pl.BlockSpec((ng, tm), lambda i, j: (0, i)),
        ],
        out_shape=[
            jax.ShapeDtypeStruct((t, nm), BF16),
            jax.ShapeDtypeStruct((t, N_GATE_COLS), F32),
            jax.ShapeDtypeStruct((ng, t), F32),
        ],
        scratch_shapes=[
            pltpu.VMEM((tm, d), BF16),
            pltpu.VMEM((tn // HEAD_DIM, tm + SUBLANES, HEAD_DIM), F32),
            pltpu.VMEM((3 * q_tiles, SUBLANES, tn), F32),
        ],
        compiler_params=_params(("arbitrary", "arbitrary")),
        name="inproj",
    )(x2, norm_w, w_main, w_gate, conv_w)


def _segmented_cumsum(x, axis, reverse=False):
    n = x.shape[axis]
    pos = lax.broadcasted_iota(jnp.int32, x.shape, axis) % CHUNK
    shift = 1
    while shift < CHUNK:
        if reverse:
            x = x + jnp.where(pos < CHUNK - shift, pltpu.roll(x, n - shift, axis), 0.0)
        else:
            x = x + jnp.where(pos >= shift, pltpu.roll(x, shift, axis), 0.0)
        shift *= 2
    return x


def _delta_rule_heads(heads, states):
    r, c, nc = GDN_ROWS, CHUNK, CHUNKS_PER_STEP
    row = lax.broadcasted_iota(jnp.int32, (r, r), 0)
    col = lax.broadcasted_iota(jnp.int32, (r, r), 1)
    same_chunk = (row // c) == (col // c)
    causal = same_chunk & (row >= col)
    strict = same_chunk & (row > col)
    eye_cat = (lax.broadcasted_iota(jnp.int32, (c, r), 0)
               == lax.broadcasted_iota(jnp.int32, (c, r), 1) % c).astype(F32)
    chunk_mask = same_chunk.astype(F32).astype(BF16)

    def pack(x_bd):
        out = x_bd[0:c]
        for i in range(1, nc):
            out = out + x_bd[i * c:(i + 1) * c]
        return out

    def unpack(x_cat):
        return jnp.concatenate([x_cat.astype(BF16)] * nc, axis=0) * chunk_mask

    def dot(a, b):
        return jnp.dot(a, b, preferred_element_type=F32)

    qns, kns, attns, n_bds = [], [], [], []
    for qn_b, kn_b, vc, beta_c, gcc, gtail, gcr in heads:
        qn = qn_b.astype(F32)
        kn = kn_b.astype(F32)
        diff = gcc - gcr
        decay = jnp.exp(jnp.where(causal, diff, MASKED_LOG_DECAY))
        decay_strict = jnp.exp(jnp.where(strict, diff, MASKED_LOG_DECAY))
        kq = jnp.concatenate([(kn * (-beta_c)).astype(BF16), qn_b], axis=0)
        gram = lax.dot_general(kq, kn_b, _NT, preferred_element_type=F32)
        qns.append(qn)
        kns.append(kn)
        attns.append((gram[r:] * decay).astype(BF16))
        n_bds.append(gram[:r] * decay_strict)

    n_cats = [pack(n_bd) for n_bd in n_bds]
    ps = [eye_cat + n_cat for n_cat in n_cats]
    w_cats = [dot(n_cat.astype(BF16), n_bd.astype(BF16)) for n_cat, n_bd in zip(n_cats, n_bds)]
    power = 2
    while power < c // 2:
        xs = [dot(jnp.concatenate([p, w_cat], axis=0).astype(BF16), unpack(w_cat))
              for p, w_cat in zip(ps, w_cats)]
        ps = [p + x[:c] for p, x in zip(ps, xs)]
        w_cats = [x[c:] for x in xs]
        power *= 2
    ps = [p + dot(p.astype(BF16), unpack(w_cat)) for p, w_cat in zip(ps, w_cats)]

    uws = []
    for (qn_b, kn_b, vc, beta_c, gcc, gtail, gcr), kn, p in zip(heads, kns, ps):
        rhs = jnp.concatenate([vc * beta_c, kn * (beta_c * jnp.exp(gcc))], axis=1).astype(BF16)
        uws.append(dot(unpack(p), rhs).astype(BF16))
    auws = [dot(attn, uw) for attn, uw in zip(attns, uws)]
    q_effs = [qn * jnp.exp(h[4]) - auw[:, HEAD_DIM:] for qn, h, auw in zip(qns, heads, auws)]
    kes = [(kn * jnp.exp(h[5])).astype(BF16) for kn, h in zip(kns, heads)]

    outs = [[] for _ in heads]
    states = list(states)
    for i in range(nc):
        rows = slice(i * c, (i + 1) * c)
        ebs = [lax.dot_general(ke[rows], uw[rows], _TN, preferred_element_type=F32)
               for ke, uw in zip(kes, uws)]
        pos = [dot(jnp.concatenate([eb[:, HEAD_DIM:], q_eff[rows]], axis=0).astype(BF16), s.astype(BF16))
               for eb, q_eff, s in zip(ebs, q_effs, states)]
        for n, (po, eb, auw, h) in enumerate(zip(pos, ebs, auws, heads)):
            outs[n].append(po[HEAD_DIM:] + auw[rows, :HEAD_DIM])
            dec = jnp.exp(h[6][:, i * c + c - 1:i * c + c])
            states[n] = dec * states[n] - po[:HEAD_DIM] + eb[:, :HEAD_DIM]
    return [jnp.concatenate(o, axis=0) for o in outs], states


def _gdn_body(q_ref, k_ref, v_ref, z_ref, g_ref, gt_ref, al_ref, dt_ref, alr_ref, dtr_ref, nw_ref, o_ref,
              s_ref, grow_ref, *, heads_per_step, n_heads):
    r = GDN_ROWS
    first = pl.program_id(2) == 0

    @pl.when(first)
    def _():
        s_ref[...] = jnp.zeros(s_ref.shape, F32)

    graw = g_ref[...]
    beta_all = jax.nn.sigmoid(graw)
    g_all = -jnp.exp(al_ref[...]) * _softplus(graw + dt_ref[...])
    gc_all = _segmented_cumsum(g_all, 0)
    gtail_all = _segmented_cumsum(g_all, 0, reverse=True) - g_all
    gtraw = gt_ref[...]
    reps = r // N_GATE_COLS
    alr = jnp.concatenate([alr_ref[...]] * reps, axis=1)
    dtr = jnp.concatenate([dtr_ref[...]] * reps, axis=1)
    grow_ref[...] = _segmented_cumsum(-jnp.exp(alr) * _softplus(gtraw + dtr), 1)

    lane = lax.broadcasted_iota(jnp.int32, (r, N_GATE_COLS), 1)
    nw = nw_ref[...]
    heads = []
    for hh in range(heads_per_step):
        head = pl.program_id(1) * heads_per_step + hh
        cols = slice(hh * HEAD_DIM, (hh + 1) * HEAD_DIM)
        beta_c = jnp.sum(jnp.where(lane == head, beta_all, 0.0), axis=1, keepdims=True)
        gcc = jnp.sum(jnp.where(lane == head + n_heads, gc_all, 0.0), axis=1, keepdims=True)
        gtail = jnp.sum(jnp.where(lane == head + n_heads, gtail_all, 0.0), axis=1, keepdims=True)
        gcr = grow_ref[pl.ds(head + n_heads, 1), :]
        heads.append((q_ref[:, cols], k_ref[:, cols], v_ref[:, cols].astype(F32), beta_c, gcc, gtail, gcr))
    outs, states = _delta_rule_heads(heads, [s_ref[hh] for hh in range(heads_per_step)])
    for hh in range(heads_per_step):
        cols = slice(hh * HEAD_DIM, (hh + 1) * HEAD_DIM)
        s_ref[hh] = states[hh]
        o_ref[:, cols] = (_rms(outs[hh], nw) * _silu(z_ref[:, cols].astype(F32))).astype(BF16)


def _gdn(proj, gates, gates_t, al, dt, alr, dtr, norm_w, *, layer, batch, seq, d_lin, heads_per_step):
    t = proj.shape[0]
    n_heads = d_lin // HEAD_DIM
    w = heads_per_step * HEAD_DIM
    nblk = d_lin // w
    r = GDN_ROWS
    steps = seq // r
    row_map = lambda b, g, i: b * steps + i
    ng = gates_t.shape[0]

    def col_spec(offset_blocks):
        return pl.BlockSpec((r, w), lambda b, g, i: (row_map(b, g, i), offset_blocks + g))

    small = lambda shape: pl.BlockSpec((None,) + shape, lambda b, g, i: (layer, 0, 0))
    body = functools.partial(_gdn_body, heads_per_step=heads_per_step, n_heads=n_heads)
    return pl.pallas_call(
        body,
        grid=(batch, nblk, steps),
        in_specs=[
            col_spec(0), col_spec(nblk), col_spec(2 * nblk), col_spec(3 * nblk),
            pl.BlockSpec((r, N_GATE_COLS), lambda b, g, i: (row_map(b, g, i), 0)),
            pl.BlockSpec((ng, r), lambda b, g, i: (0, row_map(b, g, i))),
            small((1, N_GATE_COLS)), small((1, N_GATE_COLS)),
            small((ng, N_GATE_COLS)), small((ng, N_GATE_COLS)),
            small((1, HEAD_DIM)),
        ],
        out_specs=pl.BlockSpec((r, w), lambda b, g, i: (row_map(b, g, i), g)),
        out_shape=jax.ShapeDtypeStruct((t, d_lin), BF16),
        scratch_shapes=[
            pltpu.VMEM((heads_per_step, HEAD_DIM, HEAD_DIM), F32),
            pltpu.VMEM((ng, r), F32),
        ],
        compiler_params=_params(("arbitrary", "arbitrary", "arbitrary")),
        name="gdn",
    )(proj, proj, proj, proj, gates, gates_t, al, dt, alr, dtr, norm_w)


def _mix_body(p_ref, ga_ref, gb_ref, ya_ref, x_ref, pw_ref, ps_ref, wo_ref, o_ref,
              pad_ref, mixed_ref, *, tiles_per_seq):
    tm = x_ref.shape[0]
    gdim = pw_ref.shape[1]
    seq_tile = pl.program_id(0) % tiles_per_seq
    nslab = pad_ref.shape[0]
    per_group = gdim // LANES

    @pl.when(seq_tile == 0)
    def _():
        pad_ref[:, 0:POOL_HALO, :] = jnp.zeros((nslab, POOL_HALO, LANES), F32)

    pos = seq_tile * tm + lax.broadcasted_iota(jnp.int32, (tm, 1), 0)
    for g, win in enumerate(POOL_WINDOWS):
        cols = slice(g * gdim, (g + 1) * gdim)
        cnt = jnp.minimum(pos + 1, win).astype(F32)
        pooled = []
        for c in range(g * per_group, (g + 1) * per_group):
            u = p_ref[:, c * LANES:(c + 1) * LANES].astype(F32)
            pad_ref[c, POOL_HALO:POOL_HALO + tm, :] = u
            acc = u
            for sft in range(1, win):
                acc = acc + pad_ref[c, pl.ds(POOL_HALO - sft, tm), :]
            pad_ref[c, 0:POOL_HALO, :] = pad_ref[c, tm:tm + POOL_HALO, :]
            pooled.append((acc / cnt - u).astype(BF16))
        pooled = jnp.concatenate(pooled, axis=1) if per_group > 1 else pooled[0]
        yb = jnp.dot(pooled, pw_ref[g], preferred_element_type=F32) * ps_ref[:, cols]
        mixed = (jax.nn.sigmoid(ga_ref[:, cols].astype(F32)) * ya_ref[:, cols].astype(F32)
                 + jax.nn.sigmoid(gb_ref[:, cols].astype(F32)) * yb)
        mixed_ref[:, cols] = mixed.astype(BF16)
    o_ref[...] = x_ref[...] + jnp.dot(mixed_ref[...], wo_ref[...], preferred_element_type=F32)


def _mix(proj, ya, x2, pool_w, pool_scale, w_out, *, layer, seq, tm):
    t, d = x2.shape
    _, ngroups, gdim, _ = pool_w.shape
    assert gdim % LANES == 0
    first_blk = proj.shape[1] // d - 3
    row = lambda i: (i, 0)
    const1 = pl.Buffered(1)
    return pl.pallas_call(
        functools.partial(_mix_body, tiles_per_seq=seq // tm),
        grid=(t // tm,),
        in_specs=[
            pl.BlockSpec((tm, d), lambda i: (i, first_blk)),
            pl.BlockSpec((tm, d), lambda i: (i, first_blk + 1)),
            pl.BlockSpec((tm, d), lambda i: (i, first_blk + 2)),
            pl.BlockSpec((tm, d), row),
            pl.BlockSpec((tm, d), row),
            pl.BlockSpec((None, ngroups, gdim, gdim), lambda i: (layer, 0, 0, 0), pipeline_mode=const1),
            pl.BlockSpec((None, 1, d), lambda i: (layer, 0, 0), pipeline_mode=const1),
            pl.BlockSpec((None, d, d), lambda i: (layer, 0, 0), pipeline_mode=const1),
        ],
        out_specs=pl.BlockSpec((tm, d), row),
        out_shape=jax.ShapeDtypeStruct((t, d), F32),
        scratch_shapes=[pltpu.VMEM((d // LANES, tm + POOL_HALO, LANES), F32), pltpu.VMEM((tm, d), BF16)],
        compiler_params=_params(("arbitrary",)),
        name="mix",
    )(proj, proj, proj, ya, x2, pool_w, pool_scale, w_out)


def _ffn_body(x_ref, nw_ref, wg_ref, wu_ref, cw_ref, cb_ref, wd_ref, fw_ref, o_ref,
              h_ref, pad_ref, carry_ref, *, tiles_per_seq, final_norm):
    tm = x_ref.shape[0]
    j = pl.program_id(1)
    seq_start = pl.program_id(0) % tiles_per_seq == 0
    nslab = pad_ref.shape[0]

    @pl.when(j == 0)
    def _():
        x = x_ref[...]
        h_ref[...] = _rms(x, nw_ref[...]).astype(BF16)
        o_ref[...] = x

    @pl.when(seq_start)
    def _():
        pad_ref[:, 0:SUBLANES, :] = jnp.zeros((nslab, SUBLANES, LANES), F32)

    @pl.when(jnp.logical_not(seq_start))
    def _():
        for c in range(nslab):
            pad_ref[c, 0:SUBLANES, :] = carry_ref[j, :, c * LANES:(c + 1) * LANES]

    rm = tm // FFN_ROW_BLOCKS
    base = SUBLANES - (CONV_FFN_TAPS - 1)
    ups = []
    for r in range(FFN_ROW_BLOCKS):
        h = h_ref[r * rm:(r + 1) * rm, :]
        gate = jnp.dot(h, wg_ref[...], preferred_element_type=F32)
        ups.append(jnp.dot(h, wu_ref[...], preferred_element_type=F32))
        for c in range(nslab):
            pad_ref[c, SUBLANES + r * rm:SUBLANES + (r + 1) * rm, :] = gate[:, c * LANES:(c + 1) * LANES]
        if r == FFN_ROW_BLOCKS - 1:
            carry_ref[j] = gate[rm - SUBLANES:]
    for r in range(FFN_ROW_BLOCKS):
        acts = []
        for c in range(nslab):
            cols = slice(c * LANES, (c + 1) * LANES)
            conv = cb_ref[:, cols] + cw_ref[0:1, cols] * pad_ref[c, pl.ds(base + r * rm, rm), :]
            for k in range(1, CONV_FFN_TAPS):
                conv = conv + cw_ref[k:k + 1, cols] * pad_ref[c, pl.ds(base + k + r * rm, rm), :]
            acts.append((0.5 * conv * (1.0 + lax.erf(conv * (2.0 ** -0.5))) * ups[r][:, cols]).astype(BF16))
        act = jnp.concatenate(acts, axis=1)
        o_ref[r * rm:(r + 1) * rm, :] += jnp.dot(act, wd_ref[...], preferred_element_type=F32)

    if final_norm:
        @pl.when(j == pl.num_programs(1) - 1)
        def _():
            o_ref[...] = _rms(o_ref[...], fw_ref[...])


def _ffn(x2, norm_w, w_up, conv_w, conv_b, w_down, final_w, *, layer, final_norm, seq, tm, tf):
    t, d = x2.shape
    dff = w_down.shape[1]
    nf = dff // tf
    return pl.pallas_call(
        functools.partial(_ffn_body, tiles_per_seq=seq // tm, final_norm=final_norm),
        grid=(t // tm, nf),
        in_specs=[
            pl.BlockSpec((tm, d), lambda i, j: (i, 0)),
            pl.BlockSpec((None, 1, d), lambda i, j: (layer, 0, 0)),
            pl.BlockSpec((None, d, tf), lambda i, j: (layer, 0, j)),
            pl.BlockSpec((None, d, tf), lambda i, j: (layer, 0, nf + j)),
            pl.BlockSpec((None, CONV_FFN_TAPS, tf), lambda i, j: (layer, 0, j)),
            pl.BlockSpec((None, 1, tf), lambda i, j: (layer, 0, j)),
            pl.BlockSpec((None, tf, d), lambda i, j: (layer, j, 0)),
            pl.BlockSpec((1, d), lambda i, j: (0, 0)),
        ],
        out_specs=pl.BlockSpec((tm, d), lambda i, j: (i, 0)),
        out_shape=jax.ShapeDtypeStruct((t, d), F32),
        scratch_shapes=[
            pltpu.VMEM((tm, d), BF16),
            pltpu.VMEM((tf // LANES, tm + SUBLANES, LANES), F32),
            pltpu.VMEM((nf, SUBLANES, tf), F32),
        ],
        compiler_params=_params(("arbitrary", "arbitrary")),
        name="ffn",
    )(x2, norm_w, w_up, w_up, conv_w, conv_b, w_down, final_w)


def _tile_plan(seq, d, dff):
    return dict(
        inproj_tm=_pick_tile(seq, 1024), inproj_tn=_pick_tile(d, 1024),
        mix_tm=_pick_tile(seq, 512),
        ffn_tm=_pick_tile(seq, 1024), ffn_tf=_pick_tile(dff, 512),
        heads_per_step=min(16, d // HEAD_DIM),
    )


def kernel(x, norm_mix_w, w_in, conv_qkv_w, a_log, dt_bias, gdn_norm_w, pool_w, pool_scale, w_out,
           norm_ffn_w, w_up, conv_ffn_w, conv_ffn_b, w_down, norm_final_w):
    bsz, seq, d = x.shape
    depth = w_in.shape[0]
    n_heads = d // HEAD_DIM
    dff = w_down.shape[1]
    assert seq % GDN_ROWS == 0 and d % HEAD_DIM == 0 and 2 * n_heads <= N_GATE_COLS
    assert w_in.shape[2] == 7 * d + 2 * n_heads
    plan = _tile_plan(seq, d, dff)
    x2 = x.reshape(bsz * seq, d)
    gate_lo, gate_hi = 4 * d, 4 * d + 2 * n_heads
    lane_pad = N_GATE_COLS - 2 * n_heads

    w_up_b, w_down_b, w_out_b, pool_w_b = (w.astype(BF16) for w in (w_up, w_down, w_out, pool_w))
    row3 = lambda a: a.reshape(depth, 1, a.shape[-1])
    al = row3(jnp.pad(a_log, ((0, 0), (n_heads, lane_pad))))
    dt = row3(jnp.pad(dt_bias, ((0, 0), (n_heads, lane_pad))))
    alr = jnp.broadcast_to(jnp.pad(a_log, ((0, 0), (n_heads, 0)))[:, :, None], (depth, 2 * n_heads, N_GATE_COLS))
    dtr = jnp.broadcast_to(jnp.pad(dt_bias, ((0, 0), (n_heads, 0)))[:, :, None], (depth, 2 * n_heads, N_GATE_COLS))
    norm_mix, norm_ffn, gdn_norm, pool_sc, ffn_b = (row3(a) for a in
                                                    (norm_mix_w, norm_ffn_w, gdn_norm_w, pool_scale, conv_ffn_b))

    for l in range(depth):
        wl = w_in[l]
        w_main = jnp.concatenate([wl[:, :gate_lo], wl[:, gate_hi:]], axis=1).astype(BF16)
        w_gate = jnp.pad(wl[:, gate_lo:gate_hi].astype(BF16), ((0, 0), (0, lane_pad)))
        proj, gates, gates_t = _inproj(x2, norm_mix, w_main, w_gate, conv_qkv_w, layer=l,
                                       n_gate_rows=2 * n_heads, seq=seq, tm=plan["inproj_tm"],
                                       tn=plan["inproj_tn"])
        ya = _gdn(proj, gates, gates_t, al, dt, alr, dtr, gdn_norm, layer=l,
                  batch=bsz, seq=seq, d_lin=d, heads_per_step=plan["heads_per_step"])
        x2 = _mix(proj, ya, x2, pool_w_b, pool_sc, w_out_b, layer=l, seq=seq, tm=plan["mix_tm"])
        x2 = _ffn(x2, norm_ffn, w_up_b, conv_ffn_w, ffn_b, w_down_b, norm_final_w.reshape(1, d), layer=l,
                  final_norm=(l == depth - 1), seq=seq, tm=plan["ffn_tm"], tf=plan["ffn_tf"])
    return x2.reshape(bsz, seq, d)
```

```python
import functools

import jax
import jax.numpy as jnp
from jax import lax
from jax.experimental import pallas as pl
from jax.experimental.pallas import tpu as pltpu

F32 = jnp.float32
BF16 = jnp.bfloat16

HEAD_DIM = 128
LANES = 128
CHUNK = 64
GDN_ROWS = 256
CHUNKS_PER_STEP = GDN_ROWS // CHUNK
CONV_QKV_TAPS = 4
CONV_FFN_TAPS = 3
FFN_ROW_BLOCKS = 2
INPROJ_ROW_BLOCKS = 4
POOL_WINDOWS = (2, 4, 8, 16)
POOL_HALO = 16
SUBLANES = 8
BF16_SUBLANES = 16
N_GATE_COLS = 128
EPS = 1e-6
MASKED_LOG_DECAY = -1e30
VMEM_LIMIT_BYTES = 60 * 1024 * 1024

_NT = (((1,), (1,)), ((), ()))
_TN = (((0,), (0,)), ((), ()))


def _params(semantics):
    return pltpu.CompilerParams(dimension_semantics=semantics, vmem_limit_bytes=VMEM_LIMIT_BYTES)


def _pick_tile(n, want):
    t = min(n, want)
    while n % t:
        t //= 2
    return t


def _rms(x, w):
    return x * lax.rsqrt(jnp.mean(x * x, axis=-1, keepdims=True) + EPS) * w


def _silu(x):
    return x * jax.nn.sigmoid(x)


def _softplus(x):
    return jnp.maximum(x, 0.0) + jnp.log(1.0 + jnp.exp(-jnp.abs(x)))


def _inproj_body(x_ref, nw_ref, w_ref, wg_ref, cw_ref, o_ref, g_ref, gt_ref,
                 h_ref, pad_ref, carry_ref, *, tiles_per_seq, q_tiles):
    i, j = pl.program_id(0), pl.program_id(1)
    tm, tn = o_ref.shape

    @pl.when(j == 0)
    def _():
        h = _rms(x_ref[...], nw_ref[...]).astype(BF16)
        h_ref[...] = h
        g = lax.dot_general(h, wg_ref[0], _NT, preferred_element_type=F32)
        g_ref[...] = g
        gt_ref[...] = g.T[:gt_ref.shape[0]]

    rm = tm // INPROJ_ROW_BLOCKS

    def block_acc(r):
        return lax.dot_general(h_ref[r * rm:(r + 1) * rm, :], w_ref[0], _NT, preferred_element_type=F32)

    @pl.when(j >= 3 * q_tiles)
    def _():
        for r in range(INPROJ_ROW_BLOCKS):
            o_ref[r * rm:(r + 1) * rm, :] = block_acc(r).astype(BF16)

    @pl.when(j < 3 * q_tiles)
    def _():
        seq_start = i % tiles_per_seq == 0
        heads = tn // HEAD_DIM

        @pl.when(seq_start)
        def _():
            pad_ref[:, 0:SUBLANES, :] = jnp.zeros((heads, SUBLANES, HEAD_DIM), F32)

        @pl.when(jnp.logical_not(seq_start))
        def _():
            for hh in range(heads):
                pad_ref[hh, 0:SUBLANES, :] = carry_ref[j, :, hh * HEAD_DIM:(hh + 1) * HEAD_DIM]

        is_v = j >= 2 * q_tiles
        scale = jnp.where(j < q_tiles, HEAD_DIM ** -0.5, 1.0)
        base = SUBLANES - (CONV_QKV_TAPS - 1)
        for r in range(INPROJ_ROW_BLOCKS):
            acc = block_acc(r)
            for hh in range(heads):
                pad_ref[hh, SUBLANES + r * rm:SUBLANES + (r + 1) * rm, :] = acc[:, hh * HEAD_DIM:(hh + 1) * HEAD_DIM]
            if r == INPROJ_ROW_BLOCKS - 1:
                carry_ref[j] = acc[rm - SUBLANES:]
        for r in range(INPROJ_ROW_BLOCKS):
            for hh in range(heads):
                cols = slice(hh * HEAD_DIM, (hh + 1) * HEAD_DIM)
                y = cw_ref[0:1, cols] * pad_ref[hh, pl.ds(base + r * rm, rm), :]
                for t in range(1, CONV_QKV_TAPS):
                    y = y + cw_ref[t:t + 1, cols] * pad_ref[hh, pl.ds(base + t + r * rm, rm), :]
                y = _silu(y)
                inv = lax.rsqrt(jnp.sum(y * y, axis=-1, keepdims=True) + EPS) * scale
                o_ref[r * rm:(r + 1) * rm, cols] = (y * jnp.where(is_v, 1.0, inv)).astype(BF16)


def _inproj(x2, norm_w, w_t, conv_w, *, layer, gate_row, n_gate_rows, seq, tm, tn):
    t, d = x2.shape
    ng = n_gate_rows
    nm = w_t.shape[1] - ng
    assert d % tn == 0 and nm % tn == 0 and gate_row % tn == 0 and ng % BF16_SUBLANES == 0
    q_tiles = d // tn
    gate_tile = gate_row // tn
    body = functools.partial(_inproj_body, tiles_per_seq=seq // tm, q_tiles=q_tiles)

    def w_window(i, j):
        row = pl.multiple_of(jnp.where(j < gate_tile, j * tn, j * tn + ng), BF16_SUBLANES)
        return (layer, row, 0)

    element_block = lambda rows: (pl.Element(1), pl.Element(rows), pl.Element(d))
    return pl.pallas_call(
        body,
        grid=(t // tm, nm // tn),
        in_specs=[
            pl.BlockSpec((tm, d), lambda i, j: (i, 0)),
            pl.BlockSpec((None, 1, d), lambda i, j: (layer, 0, 0)),
            pl.BlockSpec(element_block(tn), w_window),
            pl.BlockSpec(element_block(N_GATE_COLS), lambda i, j: (layer, gate_row, 0)),
            pl.BlockSpec((None, CONV_QKV_TAPS, tn), lambda i, j: (layer, 0, jnp.minimum(j, 3 * q_tiles - 1))),
        ],
        out_specs=[
            pl.BlockSpec((tm, tn), lambda i, j: (i, j)),
            pl.BlockSpec((tm, N_GATE_COLS), lambda i, j: (i, 0)),
            pl.BlockSpec((ng, tm), lambda i, j: (0, i)),
        ],
        out_shape=[
            jax.ShapeDtypeStruct((t, nm), BF16),
            jax.ShapeDtypeStruct((t, N_GATE_COLS), F32),
            jax.ShapeDtypeStruct((ng, t), F32),
        ],
        scratch_shapes=[
            pltpu.VMEM((tm, d), BF16),
            pltpu.VMEM((tn // HEAD_DIM, tm + SUBLANES, HEAD_DIM), F32),
            pltpu.VMEM((3 * q_tiles, SUBLANES, tn), F32),
        ],
        compiler_params=_params(("arbitrary", "arbitrary")),
        name="inproj",
    )(x2, norm_w, w_t, w_t, conv_w)


def _segmented_cumsum(x, axis, reverse=False):
    n = x.shape[axis]
    pos = lax.broadcasted_iota(jnp.int32, x.shape, axis) % CHUNK
    shift = 1
    while shift < CHUNK:
        if reverse:
            x = x + jnp.where(pos < CHUNK - shift, pltpu.roll(x, n - shift, axis), 0.0)
        else:
            x = x + jnp.where(pos >= shift, pltpu.roll(x, shift, axis), 0.0)
        shift *= 2
    return x


def _delta_rule_heads(heads, states):
    r, c, nc = GDN_ROWS, CHUNK, CHUNKS_PER_STEP
    row = lax.broadcasted_iota(jnp.int32, (r, r), 0)
    col = lax.broadcasted_iota(jnp.int32, (r, r), 1)
    same_chunk = (row // c) == (col // c)
    causal = same_chunk & (row >= col)
    strict = same_chunk & (row > col)
    eye_cat = (lax.broadcasted_iota(jnp.int32, (c, r), 0)
               == lax.broadcasted_iota(jnp.int32, (c, r), 1) % c).astype(F32)
    chunk_mask = same_chunk.astype(F32).astype(BF16)

    def pack(x_bd):
        out = x_bd[0:c]
        for i in range(1, nc):
            out = out + x_bd[i * c:(i + 1) * c]
        return out

    def unpack(x_cat):
        return jnp.concatenate([x_cat.astype(BF16)] * nc, axis=0) * chunk_mask

    def dot(a, b):
        return jnp.dot(a, b, preferred_element_type=F32)

    qns, kns, attns, n_bds = [], [], [], []
    for qn_b, kn_b, vc, beta_c, gcc, gtail, gcr in heads:
        qn = qn_b.astype(F32)
        kn = kn_b.astype(F32)
        diff = gcc - gcr
        decay = jnp.exp(jnp.where(causal, diff, MASKED_LOG_DECAY))
        decay_strict = jnp.exp(jnp.where(strict, diff, MASKED_LOG_DECAY))
        kq = jnp.concatenate([(kn * (-beta_c)).astype(BF16), qn_b], axis=0)
        gram = lax.dot_general(kq, kn_b, _NT, preferred_element_type=F32)
        qns.append(qn)
        kns.append(kn)
        attns.append((gram[r:] * decay).astype(BF16))
        n_bds.append(gram[:r] * decay_strict)

    n_cats = [pack(n_bd) for n_bd in n_bds]
    ps = [eye_cat + n_cat for n_cat in n_cats]
    w_cats = [dot(n_cat.astype(BF16), n_bd.astype(BF16)) for n_cat, n_bd in zip(n_cats, n_bds)]
    power = 2
    while power < c // 2:
        xs = [dot(jnp.concatenate([p, w_cat], axis=0).astype(BF16), unpack(w_cat))
              for p, w_cat in zip(ps, w_cats)]
        ps = [p + x[:c] for p, x in zip(ps, xs)]
        w_cats = [x[c:] for x in xs]
        power *= 2
    ps = [p + dot(p.astype(BF16), unpack(w_cat)) for p, w_cat in zip(ps, w_cats)]

    uws = []
    for (qn_b, kn_b, vc, beta_c, gcc, gtail, gcr), kn, p in zip(heads, kns, ps):
        rhs = jnp.concatenate([vc * beta_c, kn * (beta_c * jnp.exp(gcc))], axis=1).astype(BF16)
        uws.append(dot(unpack(p), rhs).astype(BF16))
    auws = [dot(attn, uw) for attn, uw in zip(attns, uws)]
    q_effs = [qn * jnp.exp(h[4]) - auw[:, HEAD_DIM:] for qn, h, auw in zip(qns, heads, auws)]
    kes = [(kn * jnp.exp(h[5])).astype(BF16) for kn, h in zip(kns, heads)]

    outs = [[] for _ in heads]
    states = list(states)
    for i in range(nc):
        rows = slice(i * c, (i + 1) * c)
        ebs = [lax.dot_general(ke[rows], uw[rows], _TN, preferred_element_type=F32)
               for ke, uw in zip(kes, uws)]
        pos = [dot(jnp.concatenate([eb[:, HEAD_DIM:], q_eff[rows]], axis=0).astype(BF16), s.astype(BF16))
               for eb, q_eff, s in zip(ebs, q_effs, states)]
        for n, (po, eb, auw, h) in enumerate(zip(pos, ebs, auws, heads)):
            outs[n].append(po[HEAD_DIM:] + auw[rows, :HEAD_DIM])
            dec = jnp.exp(h[6][:, i * c + c - 1:i * c + c])
            states[n] = dec * states[n] - po[:HEAD_DIM] + eb[:, :HEAD_DIM]
    return [jnp.concatenate(o, axis=0) for o in outs], states


def _gdn_body(q_ref, k_ref, v_ref, z_ref, g_ref, gt_ref, al_ref, dt_ref, alr_ref, dtr_ref, nw_ref, o_ref,
              s_ref, grow_ref, *, heads_per_step, n_heads):
    r = GDN_ROWS
    first = pl.program_id(2) == 0

    @pl.when(first)
    def _():
        s_ref[...] = jnp.zeros(s_ref.shape, F32)

    graw = g_ref[...]
    beta_all = jax.nn.sigmoid(graw)
    g_all = -jnp.exp(al_ref[...]) * _softplus(graw + dt_ref[...])
    gc_all = _segmented_cumsum(g_all, 0)
    gtail_all = _segmented_cumsum(g_all, 0, reverse=True) - g_all
    gtraw = gt_ref[...]
    reps = r // N_GATE_COLS
    alr = jnp.concatenate([alr_ref[...]] * reps, axis=1)
    dtr = jnp.concatenate([dtr_ref[...]] * reps, axis=1)
    grow_ref[...] = _segmented_cumsum(-jnp.exp(alr) * _softplus(gtraw + dtr), 1)

    lane = lax.broadcasted_iota(jnp.int32, (r, N_GATE_COLS), 1)
    nw = nw_ref[...]
    heads = []
    for hh in range(heads_per_step):
        head = pl.program_id(1) * heads_per_step + hh
        cols = slice(hh * HEAD_DIM, (hh + 1) * HEAD_DIM)
        beta_c = jnp.sum(jnp.where(lane == head, beta_all, 0.0), axis=1, keepdims=True)
        gcc = jnp.sum(jnp.where(lane == head + n_heads, gc_all, 0.0), axis=1, keepdims=True)
        gtail = jnp.sum(jnp.where(lane == head + n_heads, gtail_all, 0.0), axis=1, keepdims=True)
        gcr = grow_ref[pl.ds(head + n_heads, 1), :]
        heads.append((q_ref[:, cols], k_ref[:, cols], v_ref[:, cols].astype(F32), beta_c, gcc, gtail, gcr))
    outs, states = _delta_rule_heads(heads, [s_ref[hh] for hh in range(heads_per_step)])
    for hh in range(heads_per_step):
        cols = slice(hh * HEAD_DIM, (hh + 1) * HEAD_DIM)
        s_ref[hh] = states[hh]
        o_ref[:, cols] = (_rms(outs[hh], nw) * _silu(z_ref[:, cols].astype(F32))).astype(BF16)


def _gdn(proj, gates, gates_t, al, dt, alr, dtr, norm_w, *, layer, batch, seq, d_lin, heads_per_step):
    t = proj.shape[0]
    n_heads = d_lin // HEAD_DIM
    w = heads_per_step * HEAD_DIM
    nblk = d_lin // w
    r = GDN_ROWS
    steps = seq // r
    row_map = lambda b, g, i: b * steps + i
    ng = gates_t.shape[0]

    def col_spec(offset_blocks):
        return pl.BlockSpec((r, w), lambda b, g, i: (row_map(b, g, i), offset_blocks + g))

    small = lambda shape: pl.BlockSpec((None,) + shape, lambda b, g, i: (layer, 0, 0))
    body = functools.partial(_gdn_body, heads_per_step=heads_per_step, n_heads=n_heads)
    return pl.pallas_call(
        body,
        grid=(batch, nblk, steps),
        in_specs=[
            col_spec(0), col_spec(nblk), col_spec(2 * nblk), col_spec(3 * nblk),
            pl.BlockSpec((r, N_GATE_COLS), lambda b, g, i: (row_map(b, g, i), 0)),
            pl.BlockSpec((ng, r), lambda b, g, i: (0, row_map(b, g, i))),
            small((1, N_GATE_COLS)), small((1, N_GATE_COLS)),
            small((ng, N_GATE_COLS)), small((ng, N_GATE_COLS)),
            small((1, HEAD_DIM)),
        ],
        out_specs=pl.BlockSpec((r, w), lambda b, g, i: (row_map(b, g, i), g)),
        out_shape=jax.ShapeDtypeStruct((t, d_lin), BF16),
        scratch_shapes=[
            pltpu.VMEM((heads_per_step, HEAD_DIM, HEAD_DIM), F32),
            pltpu.VMEM((ng, r), F32),
        ],
        compiler_params=_params(("arbitrary", "arbitrary", "arbitrary")),
        name="gdn",
    )(proj, proj, proj, proj, gates, gates_t, al, dt, alr, dtr, norm_w)


def _mix_body(p_ref, ga_ref, gb_ref, ya_ref, x_ref, pw_ref, ps_ref, wo_ref, o_ref,
              pad_ref, mixed_ref, *, tiles_per_seq):
    tm = x_ref.shape[0]
    gdim = pw_ref.shape[1]
    seq_tile = pl.program_id(0) % tiles_per_seq
    nslab = pad_ref.shape[0]
    per_group = gdim // LANES

    @pl.when(seq_tile == 0)
    def _():
        pad_ref[:, 0:POOL_HALO, :] = jnp.zeros((nslab, POOL_HALO, LANES), F32)

    pos = seq_tile * tm + lax.broadcasted_iota(jnp.int32, (tm, 1), 0)
    for g, win in enumerate(POOL_WINDOWS):
        cols = slice(g * gdim, (g + 1) * gdim)
        cnt = jnp.minimum(pos + 1, win).astype(F32)
        pooled = []
        for c in range(g * per_group, (g + 1) * per_group):
            u = p_ref[:, c * LANES:(c + 1) * LANES].astype(F32)
            pad_ref[c, POOL_HALO:POOL_HALO + tm, :] = u
            acc = u
            for sft in range(1, win):
                acc = acc + pad_ref[c, pl.ds(POOL_HALO - sft, tm), :]
            pad_ref[c, 0:POOL_HALO, :] = pad_ref[c, tm:tm + POOL_HALO, :]
            pooled.append((acc / cnt - u).astype(BF16))
        pooled = jnp.concatenate(pooled, axis=1) if per_group > 1 else pooled[0]
        yb = jnp.dot(pooled, pw_ref[g], preferred_element_type=F32) * ps_ref[:, cols]
        mixed = (jax.nn.sigmoid(ga_ref[:, cols].astype(F32)) * ya_ref[:, cols].astype(F32)
                 + jax.nn.sigmoid(gb_ref[:, cols].astype(F32)) * yb)
        mixed_ref[:, cols] = mixed.astype(BF16)
    o_ref[...] = x_ref[...] + jnp.dot(mixed_ref[...], wo_ref[...], preferred_element_type=F32)


def _mix(proj, ya, x2, pool_w, pool_scale, w_out, *, layer, seq, tm):
    t, d = x2.shape
    _, ngroups, gdim, _ = pool_w.shape
    assert gdim % LANES == 0
    first_blk = proj.shape[1] // d - 3
    row = lambda i: (i, 0)
    const1 = pl.Buffered(1)
    return pl.pallas_call(
        functools.partial(_mix_body, tiles_per_seq=seq // tm),
        grid=(t // tm,),
        in_specs=[
            pl.BlockSpec((tm, d), lambda i: (i, first_blk)),
            pl.BlockSpec((tm, d), lambda i: (i, first_blk + 1)),
            pl.BlockSpec((tm, d), lambda i: (i, first_blk + 2)),
            pl.BlockSpec((tm, d), row),
            pl.BlockSpec((tm, d), row),
            pl.BlockSpec((None, ngroups, gdim, gdim), lambda i: (layer, 0, 0, 0), pipeline_mode=const1),
            pl.BlockSpec((None, 1, d), lambda i: (layer, 0, 0), pipeline_mode=const1),
            pl.BlockSpec((None, d, d), lambda i: (layer, 0, 0), pipeline_mode=const1),
        ],
        out_specs=pl.BlockSpec((tm, d), row),
        out_shape=jax.ShapeDtypeStruct((t, d), F32),
        scratch_shapes=[pltpu.VMEM((d // LANES, tm + POOL_HALO, LANES), F32), pltpu.VMEM((tm, d), BF16)],
        compiler_params=_params(("arbitrary",)),
        name="mix",
    )(proj, proj, proj, ya, x2, pool_w, pool_scale, w_out)


def _ffn_body(x_ref, nw_ref, wg_ref, wu_ref, cw_ref, cb_ref, wd_ref, fw_ref, o_ref,
              h_ref, pad_ref, carry_ref, *, tiles_per_seq, final_norm):
    tm = x_ref.shape[0]
    j = pl.program_id(1)
    seq_start = pl.program_id(0) % tiles_per_seq == 0
    nslab = pad_ref.shape[0]

    @pl.when(j == 0)
    def _():
        x = x_ref[...]
        h_ref[...] = _rms(x, nw_ref[...]).astype(BF16)
        o_ref[...] = x

    @pl.when(seq_start)
    def _():
        pad_ref[:, 0:SUBLANES, :] = jnp.zeros((nslab, SUBLANES, LANES), F32)

    @pl.when(jnp.logical_not(seq_start))
    def _():
        for c in range(nslab):
            pad_ref[c, 0:SUBLANES, :] = carry_ref[j, :, c * LANES:(c + 1) * LANES]

    rm = tm // FFN_ROW_BLOCKS
    base = SUBLANES - (CONV_FFN_TAPS - 1)
    ups = []
    for r in range(FFN_ROW_BLOCKS):
        h = h_ref[r * rm:(r + 1) * rm, :]
        gate = jnp.dot(h, wg_ref[...], preferred_element_type=F32)
        ups.append(jnp.dot(h, wu_ref[...], preferred_element_type=F32))
        for c in range(nslab):
            pad_ref[c, SUBLANES + r * rm:SUBLANES + (r + 1) * rm, :] = gate[:, c * LANES:(c + 1) * LANES]
        if r == FFN_ROW_BLOCKS - 1:
            carry_ref[j] = gate[rm - SUBLANES:]
    for r in range(FFN_ROW_BLOCKS):
        acts = []
        for c in range(nslab):
            cols = slice(c * LANES, (c + 1) * LANES)
            conv = cb_ref[:, cols] + cw_ref[0:1, cols] * pad_ref[c, pl.ds(base + r * rm, rm), :]
            for k in range(1, CONV_FFN_TAPS):
                conv = conv + cw_ref[k:k + 1, cols] * pad_ref[c, pl.ds(base + k + r * rm, rm), :]
            acts.append((0.5 * conv * (1.0 + lax.erf(conv * (2.0 ** -0.5))) * ups[r][:, cols]).astype(BF16))
        act = jnp.concatenate(acts, axis=1)
        o_ref[r * rm:(r + 1) * rm, :] += jnp.dot(act, wd_ref[...], preferred_element_type=F32)

    if final_norm:
        @pl.when(j == pl.num_programs(1) - 1)
        def _():
            o_ref[...] = _rms(o_ref[...], fw_ref[...])


def _ffn(x2, norm_w, w_up, conv_w, conv_b, w_down, final_w, *, layer, final_norm, seq, tm, tf):
    t, d = x2.shape
    dff = w_down.shape[1]
    nf = dff // tf
    return pl.pallas_call(
        functools.partial(_ffn_body, tiles_per_seq=seq // tm, final_norm=final_norm),
        grid=(t // tm, nf),
        in_specs=[
            pl.BlockSpec((tm, d), lambda i, j: (i, 0)),
            pl.BlockSpec((None, 1, d), lambda i, j: (layer, 0, 0)),
            pl.BlockSpec((None, d, tf), lambda i, j: (layer, 0, j)),
            pl.BlockSpec((None, d, tf), lambda i, j: (layer, 0, nf + j)),
            pl.BlockSpec((None, CONV_FFN_TAPS, tf), lambda i, j: (layer, 0, j)),
            pl.BlockSpec((None, 1, tf), lambda i, j: (layer, 0, j)),
            pl.BlockSpec((None, tf, d), lambda i, j: (layer, j, 0)),
            pl.BlockSpec((1, d), lambda i, j: (0, 0)),
        ],
        out_specs=pl.BlockSpec((tm, d), lambda i, j: (i, 0)),
        out_shape=jax.ShapeDtypeStruct((t, d), F32),
        scratch_shapes=[
            pltpu.VMEM((tm, d), BF16),
            pltpu.VMEM((tf // LANES, tm + SUBLANES, LANES), F32),
            pltpu.VMEM((nf, SUBLANES, tf), F32),
        ],
        compiler_params=_params(("arbitrary", "arbitrary")),
        name="ffn",
    )(x2, norm_w, w_up, w_up, conv_w, conv_b, w_down, final_w)


def _tile_plan(seq, d, dff):
    return dict(
        inproj_tm=_pick_tile(seq, 1024), inproj_tn=_pick_tile(d, 1024),
        mix_tm=_pick_tile(seq, 512),
        ffn_tm=_pick_tile(seq, 1024), ffn_tf=_pick_tile(dff, 512),
        heads_per_step=min(16, d // HEAD_DIM),
    )


def kernel(x, norm_mix_w, w_in, conv_qkv_w, a_log, dt_bias, gdn_norm_w, pool_w, pool_scale, w_out,
           norm_ffn_w, w_up, conv_ffn_w, conv_ffn_b, w_down, norm_final_w):
    bsz, seq, d = x.shape
    depth = w_in.shape[0]
    n_heads = d // HEAD_DIM
    dff = w_down.shape[1]
    assert seq % GDN_ROWS == 0 and d % HEAD_DIM == 0 and 2 * n_heads <= N_GATE_COLS
    assert w_in.shape[2] == 7 * d + 2 * n_heads
    plan = _tile_plan(seq, d, dff)
    x2 = x.reshape(bsz * seq, d)
    gate_lo, gate_hi = 4 * d, 4 * d + 2 * n_heads
    lane_pad = N_GATE_COLS - 2 * n_heads

    w_t = jnp.swapaxes(w_in, 1, 2).astype(BF16)
    w_up_b, w_down_b, w_out_b, pool_w_b = (w.astype(BF16) for w in (w_up, w_down, w_out, pool_w))
    row3 = lambda a: a.reshape(depth, 1, a.shape[-1])
    al = row3(jnp.pad(a_log, ((0, 0), (n_heads, lane_pad))))
    dt = row3(jnp.pad(dt_bias, ((0, 0), (n_heads, lane_pad))))
    alr = jnp.broadcast_to(jnp.pad(a_log, ((0, 0), (n_heads, 0)))[:, :, None], (depth, 2 * n_heads, N_GATE_COLS))
    dtr = jnp.broadcast_to(jnp.pad(dt_bias, ((0, 0), (n_heads, 0)))[:, :, None], (depth, 2 * n_heads, N_GATE_COLS))
    norm_mix, norm_ffn, gdn_norm, pool_sc, ffn_b = (row3(a) for a in
                                                    (norm_mix_w, norm_ffn_w, gdn_norm_w, pool_scale, conv_ffn_b))

    for l in range(depth):
        proj, gates, gates_t = _inproj(x2, norm_mix, w_t, conv_qkv_w, layer=l, gate_row=gate_lo,
                                       n_gate_rows=2 * n_heads, seq=seq, tm=plan["inproj_tm"],
                                       tn=plan["inproj_tn"])
        ya = _gdn(proj, gates, gates_t, al, dt, alr, dtr, gdn_norm, layer=l,
                  batch=bsz, seq=seq, d_lin=d, heads_per_step=plan["heads_per_step"])
        x2 = _mix(proj, ya, x2, pool_w_b, pool_sc, w_out_b, layer=l, seq=seq, tm=plan["mix_tm"])
        x2 = _ffn(x2, norm_ffn, w_up_b, conv_ffn_w, ffn_b, w_down_b, norm_final_w.reshape(1, d), layer=l,
                  final_norm=(l == depth - 1), seq=seq, tm=plan["ffn_tm"], tf=plan["ffn_tf"])
    return x2.reshape(bsz, seq, d)
```

```python
import functools

import jax
import jax.numpy as jnp
from jax import lax
from jax.experimental import pallas as pl
from jax.experimental.pallas import tpu as pltpu

F32 = jnp.float32
BF16 = jnp.bfloat16

HEAD_DIM = 128
LANES = 128
CHUNK = 64
GDN_ROWS = 256
CHUNKS_PER_STEP = GDN_ROWS // CHUNK
CONV_QKV_TAPS = 4
CONV_FFN_TAPS = 3
POOL_WINDOWS = (2, 4, 8, 16)
SWEEP_INPROJ = (2, 4, 8, 1)
SWEEP_FFN = (1, 2, 4, 2)
POOL_HALO = 16
SUBLANES = 8
BF16_SUBLANES = 16
N_GATE_COLS = 128
EPS = 1e-6
MASKED_LOG_DECAY = -1e30
VMEM_LIMIT_BYTES = 60 * 1024 * 1024

_NT = (((1,), (1,)), ((), ()))
_TN = (((0,), (0,)), ((), ()))


def _params(semantics):
    return pltpu.CompilerParams(dimension_semantics=semantics, vmem_limit_bytes=VMEM_LIMIT_BYTES)


def _pick_tile(n, want):
    t = min(n, want)
    while n % t:
        t //= 2
    return t


def _rms(x, w):
    return x * lax.rsqrt(jnp.mean(x * x, axis=-1, keepdims=True) + EPS) * w


def _silu(x):
    return x * jax.nn.sigmoid(x)


def _softplus(x):
    return jnp.maximum(x, 0.0) + jnp.log(1.0 + jnp.exp(-jnp.abs(x)))


def _inproj_body(x_ref, nw_ref, w_ref, wg_ref, cw_ref, o_ref, g_ref, gt_ref,
                 h_ref, pad_ref, carry_ref, *, tiles_per_seq, q_tiles, row_blocks):
    i, j = pl.program_id(0), pl.program_id(1)
    tm, tn = o_ref.shape

    @pl.when(j == 0)
    def _():
        h = _rms(x_ref[...], nw_ref[...]).astype(BF16)
        h_ref[...] = h
        g = lax.dot_general(h, wg_ref[0], _NT, preferred_element_type=F32)
        g_ref[...] = g
        gt_ref[...] = g.T[:gt_ref.shape[0]]

    rm = tm // row_blocks

    def block_acc(r):
        return lax.dot_general(h_ref[r * rm:(r + 1) * rm, :], w_ref[0], _NT, preferred_element_type=F32)

    @pl.when(j >= 3 * q_tiles)
    def _():
        for r in range(row_blocks):
            o_ref[r * rm:(r + 1) * rm, :] = block_acc(r).astype(BF16)

    @pl.when(j < 3 * q_tiles)
    def _():
        seq_start = i % tiles_per_seq == 0
        heads = tn // HEAD_DIM

        @pl.when(seq_start)
        def _():
            pad_ref[:, 0:SUBLANES, :] = jnp.zeros((heads, SUBLANES, HEAD_DIM), F32)

        @pl.when(jnp.logical_not(seq_start))
        def _():
            for hh in range(heads):
                pad_ref[hh, 0:SUBLANES, :] = carry_ref[j, :, hh * HEAD_DIM:(hh + 1) * HEAD_DIM]

        is_v = j >= 2 * q_tiles
        scale = jnp.where(j < q_tiles, HEAD_DIM ** -0.5, 1.0)
        base = SUBLANES - (CONV_QKV_TAPS - 1)
        for r in range(row_blocks):
            acc = block_acc(r)
            for hh in range(heads):
                pad_ref[hh, SUBLANES + r * rm:SUBLANES + (r + 1) * rm, :] = acc[:, hh * HEAD_DIM:(hh + 1) * HEAD_DIM]
            if r == row_blocks - 1:
                carry_ref[j] = acc[rm - SUBLANES:]
        for r in range(row_blocks):
            for hh in range(heads):
                cols = slice(hh * HEAD_DIM, (hh + 1) * HEAD_DIM)
                y = cw_ref[0:1, cols] * pad_ref[hh, pl.ds(base + r * rm, rm), :]
                for t in range(1, CONV_QKV_TAPS):
                    y = y + cw_ref[t:t + 1, cols] * pad_ref[hh, pl.ds(base + t + r * rm, rm), :]
                y = _silu(y)
                inv = lax.rsqrt(jnp.sum(y * y, axis=-1, keepdims=True) + EPS) * scale
                o_ref[r * rm:(r + 1) * rm, cols] = (y * jnp.where(is_v, 1.0, inv)).astype(BF16)


def _inproj(x2, norm_w, w_t, conv_w, *, layer, gate_row, n_gate_rows, seq, tm, tn, row_blocks):
    t, d = x2.shape
    ng = n_gate_rows
    nm = w_t.shape[1] - ng
    assert d % tn == 0 and nm % tn == 0 and gate_row % tn == 0 and ng % BF16_SUBLANES == 0
    q_tiles = d // tn
    gate_tile = gate_row // tn
    body = functools.partial(_inproj_body, tiles_per_seq=seq // tm, q_tiles=q_tiles, row_blocks=row_blocks)

    def w_window(i, j):
        row = pl.multiple_of(jnp.where(j < gate_tile, j * tn, j * tn + ng), BF16_SUBLANES)
        return (layer, row, 0)

    element_block = lambda rows: (pl.Element(1), pl.Element(rows), pl.Element(d))
    return pl.pallas_call(
        body,
        grid=(t // tm, nm // tn),
        in_specs=[
            pl.BlockSpec((tm, d), lambda i, j: (i, 0)),
            pl.BlockSpec((None, 1, d), lambda i, j: (layer, 0, 0)),
            pl.BlockSpec(element_block(tn), w_window),
            pl.BlockSpec(element_block(N_GATE_COLS), lambda i, j: (layer, gate_row, 0)),
            pl.BlockSpec((None, CONV_QKV_TAPS, tn), lambda i, j: (layer, 0, jnp.minimum(j, 3 * q_tiles - 1))),
        ],
        out_specs=[
            pl.BlockSpec((tm, tn), lambda i, j: (i, j)),
            pl.BlockSpec((tm, N_GATE_COLS), lambda i, j: (i, 0)),
            pl.BlockSpec((ng, tm), lambda i, j: (0, i)),
        ],
        out_shape=[
            jax.ShapeDtypeStruct((t, nm), BF16),
            jax.ShapeDtypeStruct((t, N_GATE_COLS), F32),
            jax.ShapeDtypeStruct((ng, t), F32),
        ],
        scratch_shapes=[
            pltpu.VMEM((tm, d), BF16),
            pltpu.VMEM((tn // HEAD_DIM, tm + SUBLANES, HEAD_DIM), F32),
            pltpu.VMEM((3 * q_tiles, SUBLANES, tn), F32),
        ],
        compiler_params=_params(("arbitrary", "arbitrary")),
        name="inproj",
    )(x2, norm_w, w_t, w_t, conv_w)


def _segmented_cumsum(x, axis, reverse=False):
    n = x.shape[axis]
    pos = lax.broadcasted_iota(jnp.int32, x.shape, axis) % CHUNK
    shift = 1
    while shift < CHUNK:
        if reverse:
            x = x + jnp.where(pos < CHUNK - shift, pltpu.roll(x, n - shift, axis), 0.0)
        else:
            x = x + jnp.where(pos >= shift, pltpu.roll(x, shift, axis), 0.0)
        shift *= 2
    return x


def _delta_rule_heads(heads, states):
    r, c, nc = GDN_ROWS, CHUNK, CHUNKS_PER_STEP
    row = lax.broadcasted_iota(jnp.int32, (r, r), 0)
    col = lax.broadcasted_iota(jnp.int32, (r, r), 1)
    same_chunk = (row // c) == (col // c)
    causal = same_chunk & (row >= col)
    strict = same_chunk & (row > col)
    eye_cat = (lax.broadcasted_iota(jnp.int32, (c, r), 0)
               == lax.broadcasted_iota(jnp.int32, (c, r), 1) % c).astype(F32)
    chunk_mask = same_chunk.astype(F32).astype(BF16)

    def pack(x_bd):
        out = x_bd[0:c]
        for i in range(1, nc):
            out = out + x_bd[i * c:(i + 1) * c]
        return out

    def unpack(x_cat):
        return jnp.concatenate([x_cat.astype(BF16)] * nc, axis=0) * chunk_mask

    def dot(a, b):
        return jnp.dot(a, b, preferred_element_type=F32)

    qns, kns, attns, n_bds = [], [], [], []
    for qn_b, kn_b, vc, beta_c, gcc, gtail, gcr in heads:
        qn = qn_b.astype(F32)
        kn = kn_b.astype(F32)
        diff = gcc - gcr
        decay = jnp.exp(jnp.where(causal, diff, MASKED_LOG_DECAY))
        decay_strict = jnp.exp(jnp.where(strict, diff, MASKED_LOG_DECAY))
        kq = jnp.concatenate([(kn * (-beta_c)).astype(BF16), qn_b], axis=0)
        gram = lax.dot_general(kq, kn_b, _NT, preferred_element_type=F32)
        qns.append(qn)
        kns.append(kn)
        attns.append((gram[r:] * decay).astype(BF16))
        n_bds.append(gram[:r] * decay_strict)

    n_cats = [pack(n_bd) for n_bd in n_bds]
    ps = [eye_cat + n_cat for n_cat in n_cats]
    w_cats = [dot(n_cat.astype(BF16), n_bd.astype(BF16)) for n_cat, n_bd in zip(n_cats, n_bds)]
    power = 2
    while power < c // 2:
        xs = [dot(jnp.concatenate([p, w_cat], axis=0).astype(BF16), unpack(w_cat))
              for p, w_cat in zip(ps, w_cats)]
        ps = [p + x[:c] for p, x in zip(ps, xs)]
        w_cats = [x[c:] for x in xs]
        power *= 2
    ps = [p + dot(p.astype(BF16), unpack(w_cat)) for p, w_cat in zip(ps, w_cats)]

    uws = []
    for (qn_b, kn_b, vc, beta_c, gcc, gtail, gcr), kn, p in zip(heads, kns, ps):
        rhs = jnp.concatenate([vc * beta_c, kn * (beta_c * jnp.exp(gcc))], axis=1).astype(BF16)
        uws.append(dot(unpack(p), rhs).astype(BF16))
    auws = [dot(attn, uw) for attn, uw in zip(attns, uws)]
    q_effs = [qn * jnp.exp(h[4]) - auw[:, HEAD_DIM:] for qn, h, auw in zip(qns, heads, auws)]
    kes = [(kn * jnp.exp(h[5])).astype(BF16) for kn, h in zip(kns, heads)]

    outs = [[] for _ in heads]
    states = list(states)
    for i in range(nc):
        rows = slice(i * c, (i + 1) * c)
        ebs = [lax.dot_general(ke[rows], uw[rows], _TN, preferred_element_type=F32)
               for ke, uw in zip(kes, uws)]
        pos = [dot(jnp.concatenate([eb[:, HEAD_DIM:], q_eff[rows]], axis=0).astype(BF16), s.astype(BF16))
               for eb, q_eff, s in zip(ebs, q_effs, states)]
        for n, (po, eb, auw, h) in enumerate(zip(pos, ebs, auws, heads)):
            outs[n].append(po[HEAD_DIM:] + auw[rows, :HEAD_DIM])
            dec = jnp.exp(h[6][:, i * c + c - 1:i * c + c])
            states[n] = dec * states[n] - po[:HEAD_DIM] + eb[:, :HEAD_DIM]
    return [jnp.concatenate(o, axis=0) for o in outs], states


def _gdn_body(q_ref, k_ref, v_ref, z_ref, g_ref, gt_ref, al_ref, dt_ref, alr_ref, dtr_ref, nw_ref, o_ref,
              s_ref, grow_ref, *, heads_per_step, n_heads):
    r = GDN_ROWS
    first = pl.program_id(2) == 0

    @pl.when(first)
    def _():
        s_ref[...] = jnp.zeros(s_ref.shape, F32)

    graw = g_ref[...]
    beta_all = jax.nn.sigmoid(graw)
    g_all = -jnp.exp(al_ref[...]) * _softplus(graw + dt_ref[...])
    gc_all = _segmented_cumsum(g_all, 0)
    gtail_all = _segmented_cumsum(g_all, 0, reverse=True) - g_all
    gtraw = gt_ref[...]
    reps = r // N_GATE_COLS
    alr = jnp.concatenate([alr_ref[...]] * reps, axis=1)
    dtr = jnp.concatenate([dtr_ref[...]] * reps, axis=1)
    grow_ref[...] = _segmented_cumsum(-jnp.exp(alr) * _softplus(gtraw + dtr), 1)

    lane = lax.broadcasted_iota(jnp.int32, (r, N_GATE_COLS), 1)
    nw = nw_ref[...]
    heads = []
    for hh in range(heads_per_step):
        head = pl.program_id(1) * heads_per_step + hh
        cols = slice(hh * HEAD_DIM, (hh + 1) * HEAD_DIM)
        beta_c = jnp.sum(jnp.where(lane == head, beta_all, 0.0), axis=1, keepdims=True)
        gcc = jnp.sum(jnp.where(lane == head + n_heads, gc_all, 0.0), axis=1, keepdims=True)
        gtail = jnp.sum(jnp.where(lane == head + n_heads, gtail_all, 0.0), axis=1, keepdims=True)
        gcr = grow_ref[pl.ds(head + n_heads, 1), :]
        heads.append((q_ref[:, cols], k_ref[:, cols], v_ref[:, cols].astype(F32), beta_c, gcc, gtail, gcr))
    outs, states = _delta_rule_heads(heads, [s_ref[hh] for hh in range(heads_per_step)])
    for hh in range(heads_per_step):
        cols = slice(hh * HEAD_DIM, (hh + 1) * HEAD_DIM)
        s_ref[hh] = states[hh]
        o_ref[:, cols] = (_rms(outs[hh], nw) * _silu(z_ref[:, cols].astype(F32))).astype(BF16)


def _gdn(proj, gates, gates_t, al, dt, alr, dtr, norm_w, *, layer, batch, seq, d_lin, heads_per_step):
    t = proj.shape[0]
    n_heads = d_lin // HEAD_DIM
    w = heads_per_step * HEAD_DIM
    nblk = d_lin // w
    r = GDN_ROWS
    steps = seq // r
    row_map = lambda b, g, i: b * steps + i
    ng = gates_t.shape[0]

    def col_spec(offset_blocks):
        return pl.BlockSpec((r, w), lambda b, g, i: (row_map(b, g, i), offset_blocks + g))

    small = lambda shape: pl.BlockSpec((None,) + shape, lambda b, g, i: (layer, 0, 0))
    body = functools.partial(_gdn_body, heads_per_step=heads_per_step, n_heads=n_heads)
    return pl.pallas_call(
        body,
        grid=(batch, nblk, steps),
        in_specs=[
            col_spec(0), col_spec(nblk), col_spec(2 * nblk), col_spec(3 * nblk),
            pl.BlockSpec((r, N_GATE_COLS), lambda b, g, i: (row_map(b, g, i), 0)),
            pl.BlockSpec((ng, r), lambda b, g, i: (0, row_map(b, g, i))),
            small((1, N_GATE_COLS)), small((1, N_GATE_COLS)),
            small((ng, N_GATE_COLS)), small((ng, N_GATE_COLS)),
            small((1, HEAD_DIM)),
        ],
        out_specs=pl.BlockSpec((r, w), lambda b, g, i: (row_map(b, g, i), g)),
        out_shape=jax.ShapeDtypeStruct((t, d_lin), BF16),
        scratch_shapes=[
            pltpu.VMEM((heads_per_step, HEAD_DIM, HEAD_DIM), F32),
            pltpu.VMEM((ng, r), F32),
        ],
        compiler_params=_params(("arbitrary", "arbitrary", "arbitrary")),
        name="gdn",
    )(proj, proj, proj, proj, gates, gates_t, al, dt, alr, dtr, norm_w)


def _mix_body(p_ref, ga_ref, gb_ref, ya_ref, x_ref, pw_ref, ps_ref, wo_ref, o_ref,
              pad_ref, mixed_ref, *, tiles_per_seq):
    tm = x_ref.shape[0]
    gdim = pw_ref.shape[1]
    seq_tile = pl.program_id(0) % tiles_per_seq
    nslab = pad_ref.shape[0]
    per_group = gdim // LANES

    @pl.when(seq_tile == 0)
    def _():
        pad_ref[:, 0:POOL_HALO, :] = jnp.zeros((nslab, POOL_HALO, LANES), F32)

    pos = seq_tile * tm + lax.broadcasted_iota(jnp.int32, (tm, 1), 0)
    for g, win in enumerate(POOL_WINDOWS):
        cols = slice(g * gdim, (g + 1) * gdim)
        cnt = jnp.minimum(pos + 1, win).astype(F32)
        pooled = []
        for c in range(g * per_group, (g + 1) * per_group):
            u = p_ref[:, c * LANES:(c + 1) * LANES].astype(F32)
            pad_ref[c, POOL_HALO:POOL_HALO + tm, :] = u
            acc = u
            for sft in range(1, win):
                acc = acc + pad_ref[c, pl.ds(POOL_HALO - sft, tm), :]
            pad_ref[c, 0:POOL_HALO, :] = pad_ref[c, tm:tm + POOL_HALO, :]
            pooled.append((acc / cnt - u).astype(BF16))
        pooled = jnp.concatenate(pooled, axis=1) if per_group > 1 else pooled[0]
        yb = jnp.dot(pooled, pw_ref[g], preferred_element_type=F32) * ps_ref[:, cols]
        mixed = (jax.nn.sigmoid(ga_ref[:, cols].astype(F32)) * ya_ref[:, cols].astype(F32)
                 + jax.nn.sigmoid(gb_ref[:, cols].astype(F32)) * yb)
        mixed_ref[:, cols] = mixed.astype(BF16)
    o_ref[...] = x_ref[...] + jnp.dot(mixed_ref[...], wo_ref[...], preferred_element_type=F32)


def _mix(proj, ya, x2, pool_w, pool_scale, w_out, *, layer, seq, tm):
    t, d = x2.shape
    _, ngroups, gdim, _ = pool_w.shape
    assert gdim % LANES == 0
    first_blk = proj.shape[1] // d - 3
    row = lambda i: (i, 0)
    const1 = pl.Buffered(1)
    return pl.pallas_call(
        functools.partial(_mix_body, tiles_per_seq=seq // tm),
        grid=(t // tm,),
        in_specs=[
            pl.BlockSpec((tm, d), lambda i: (i, first_blk)),
            pl.BlockSpec((tm, d), lambda i: (i, first_blk + 1)),
            pl.BlockSpec((tm, d), lambda i: (i, first_blk + 2)),
            pl.BlockSpec((tm, d), row),
            pl.BlockSpec((tm, d), row),
            pl.BlockSpec((None, ngroups, gdim, gdim), lambda i: (layer, 0, 0, 0), pipeline_mode=const1),
            pl.BlockSpec((None, 1, d), lambda i: (layer, 0, 0), pipeline_mode=const1),
            pl.BlockSpec((None, d, d), lambda i: (layer, 0, 0), pipeline_mode=const1),
        ],
        out_specs=pl.BlockSpec((tm, d), row),
        out_shape=jax.ShapeDtypeStruct((t, d), F32),
        scratch_shapes=[pltpu.VMEM((d // LANES, tm + POOL_HALO, LANES), F32), pltpu.VMEM((tm, d), BF16)],
        compiler_params=_params(("arbitrary",)),
        name="mix",
    )(proj, proj, proj, ya, x2, pool_w, pool_scale, w_out)


def _ffn_body(x_ref, nw_ref, wg_ref, wu_ref, cw_ref, cb_ref, wd_ref, fw_ref, o_ref,
              h_ref, pad_ref, carry_ref, *, tiles_per_seq, final_norm, row_blocks):
    tm = x_ref.shape[0]
    j = pl.program_id(1)
    seq_start = pl.program_id(0) % tiles_per_seq == 0
    nslab = pad_ref.shape[0]

    @pl.when(j == 0)
    def _():
        x = x_ref[...]
        h_ref[...] = _rms(x, nw_ref[...]).astype(BF16)
        o_ref[...] = x

    @pl.when(seq_start)
    def _():
        pad_ref[:, 0:SUBLANES, :] = jnp.zeros((nslab, SUBLANES, LANES), F32)

    @pl.when(jnp.logical_not(seq_start))
    def _():
        for c in range(nslab):
            pad_ref[c, 0:SUBLANES, :] = carry_ref[j, :, c * LANES:(c + 1) * LANES]

    rm = tm // row_blocks
    base = SUBLANES - (CONV_FFN_TAPS - 1)
    ups = []
    for r in range(row_blocks):
        h = h_ref[r * rm:(r + 1) * rm, :]
        gate = jnp.dot(h, wg_ref[...], preferred_element_type=F32)
        ups.append(jnp.dot(h, wu_ref[...], preferred_element_type=F32))
        for c in range(nslab):
            pad_ref[c, SUBLANES + r * rm:SUBLANES + (r + 1) * rm, :] = gate[:, c * LANES:(c + 1) * LANES]
        if r == row_blocks - 1:
            carry_ref[j] = gate[rm - SUBLANES:]
    for r in range(row_blocks):
        acts = []
        for c in range(nslab):
            cols = slice(c * LANES, (c + 1) * LANES)
            conv = cb_ref[:, cols] + cw_ref[0:1, cols] * pad_ref[c, pl.ds(base + r * rm, rm), :]
            for k in range(1, CONV_FFN_TAPS):
                conv = conv + cw_ref[k:k + 1, cols] * pad_ref[c, pl.ds(base + k + r * rm, rm), :]
            acts.append((0.5 * conv * (1.0 + lax.erf(conv * (2.0 ** -0.5))) * ups[r][:, cols]).astype(BF16))
        act = jnp.concatenate(acts, axis=1)
        o_ref[r * rm:(r + 1) * rm, :] += jnp.dot(act, wd_ref[...], preferred_element_type=F32)

    if final_norm:
        @pl.when(j == pl.num_programs(1) - 1)
        def _():
            o_ref[...] = _rms(o_ref[...], fw_ref[...])


def _ffn(x2, norm_w, w_up, conv_w, conv_b, w_down, final_w, *, layer, final_norm, seq, tm, tf, row_blocks):
    t, d = x2.shape
    dff = w_down.shape[1]
    nf = dff // tf
    return pl.pallas_call(
        functools.partial(_ffn_body, tiles_per_seq=seq // tm, final_norm=final_norm, row_blocks=row_blocks),
        grid=(t // tm, nf),
        in_specs=[
            pl.BlockSpec((tm, d), lambda i, j: (i, 0)),
            pl.BlockSpec((None, 1, d), lambda i, j: (layer, 0, 0)),
            pl.BlockSpec((None, d, tf), lambda i, j: (layer, 0, j)),
            pl.BlockSpec((None, d, tf), lambda i, j: (layer, 0, nf + j)),
            pl.BlockSpec((None, CONV_FFN_TAPS, tf), lambda i, j: (layer, 0, j)),
            pl.BlockSpec((None, 1, tf), lambda i, j: (layer, 0, j)),
            pl.BlockSpec((None, tf, d), lambda i, j: (layer, j, 0)),
            pl.BlockSpec((1, d), lambda i, j: (0, 0)),
        ],
        out_specs=pl.BlockSpec((tm, d), lambda i, j: (i, 0)),
        out_shape=jax.ShapeDtypeStruct((t, d), F32),
        scratch_shapes=[
            pltpu.VMEM((tm, d), BF16),
            pltpu.VMEM((tf // LANES, tm + SUBLANES, LANES), F32),
            pltpu.VMEM((nf, SUBLANES, tf), F32),
        ],
        compiler_params=_params(("arbitrary", "arbitrary")),
        name="ffn",
    )(x2, norm_w, w_up, w_up, conv_w, conv_b, w_down, final_w)


def _tile_plan(seq, d, dff):
    return dict(
        inproj_tm=_pick_tile(seq, 1024), inproj_tn=_pick_tile(d, 1024),
        mix_tm=_pick_tile(seq, 512),
        ffn_tm=_pick_tile(seq, 1024), ffn_tf=_pick_tile(dff, 512),
        heads_per_step=min(16, d // HEAD_DIM),
    )


def kernel(x, norm_mix_w, w_in, conv_qkv_w, a_log, dt_bias, gdn_norm_w, pool_w, pool_scale, w_out,
           norm_ffn_w, w_up, conv_ffn_w, conv_ffn_b, w_down, norm_final_w):
    bsz, seq, d = x.shape
    depth = w_in.shape[0]
    n_heads = d // HEAD_DIM
    dff = w_down.shape[1]
    assert seq % GDN_ROWS == 0 and d % HEAD_DIM == 0 and 2 * n_heads <= N_GATE_COLS
    assert w_in.shape[2] == 7 * d + 2 * n_heads
    plan = _tile_plan(seq, d, dff)
    x2 = x.reshape(bsz * seq, d)
    gate_lo, gate_hi = 4 * d, 4 * d + 2 * n_heads
    lane_pad = N_GATE_COLS - 2 * n_heads

    w_t = jnp.swapaxes(w_in, 1, 2).astype(BF16)
    w_up_b, w_down_b, w_out_b, pool_w_b = (w.astype(BF16) for w in (w_up, w_down, w_out, pool_w))
    row3 = lambda a: a.reshape(depth, 1, a.shape[-1])
    al = row3(jnp.pad(a_log, ((0, 0), (n_heads, lane_pad))))
    dt = row3(jnp.pad(dt_bias, ((0, 0), (n_heads, lane_pad))))
    alr = jnp.broadcast_to(jnp.pad(a_log, ((0, 0), (n_heads, 0)))[:, :, None], (depth, 2 * n_heads, N_GATE_COLS))
    dtr = jnp.broadcast_to(jnp.pad(dt_bias, ((0, 0), (n_heads, 0)))[:, :, None], (depth, 2 * n_heads, N_GATE_COLS))
    norm_mix, norm_ffn, gdn_norm, pool_sc, ffn_b = (row3(a) for a in
                                                    (norm_mix_w, norm_ffn_w, gdn_norm_w, pool_scale, conv_ffn_b))

    for l in range(depth):
        proj, gates, gates_t = _inproj(x2, norm_mix, w_t, conv_qkv_w, layer=l, gate_row=gate_lo,
                                       n_gate_rows=2 * n_heads, seq=seq, tm=plan["inproj_tm"],
                                       tn=plan["inproj_tn"], row_blocks=SWEEP_INPROJ[l])
        ya = _gdn(proj, gates, gates_t, al, dt, alr, dtr, gdn_norm, layer=l,
                  batch=bsz, seq=seq, d_lin=d, heads_per_step=plan["heads_per_step"])
        x2 = _mix(proj, ya, x2, pool_w_b, pool_sc, w_out_b, layer=l, seq=seq, tm=plan["mix_tm"])
        x2 = _ffn(x2, norm_ffn, w_up_b, conv_ffn_w, ffn_b, w_down_b, norm_final_w.reshape(1, d), layer=l,
                  final_norm=(l == depth - 1), seq=seq, tm=plan["ffn_tm"], tf=plan["ffn_tf"],
                  row_blocks=SWEEP_FFN[l])
    return x2.reshape(bsz, seq, d)
```

```python
import functools

import jax
import jax.numpy as jnp
from jax import lax
from jax.experimental import pallas as pl
from jax.experimental.pallas import tpu as pltpu

F32 = jnp.float32
BF16 = jnp.bfloat16

HEAD_DIM = 128
LANES = 128
CHUNK = 64
GDN_ROWS = 256
CHUNKS_PER_STEP = GDN_ROWS // CHUNK
CONV_QKV_TAPS = 4
CONV_FFN_TAPS = 3
POOL_WINDOWS = (2, 4, 8, 16)
POOL_HALO = 16
SUBLANES = 8
BF16_SUBLANES = 16
N_GATE_COLS = 128
EPS = 1e-6
MASKED_LOG_DECAY = -1e30
VMEM_LIMIT_BYTES = 60 * 1024 * 1024

_NT = (((1,), (1,)), ((), ()))
_TN = (((0,), (0,)), ((), ()))


def _params(semantics):
    return pltpu.CompilerParams(dimension_semantics=semantics, vmem_limit_bytes=VMEM_LIMIT_BYTES)


def _pick_tile(n, want):
    t = min(n, want)
    while n % t:
        t //= 2
    return t


def _rms(x, w):
    return x * lax.rsqrt(jnp.mean(x * x, axis=-1, keepdims=True) + EPS) * w


def _silu(x):
    return x * jax.nn.sigmoid(x)


def _softplus(x):
    return jnp.maximum(x, 0.0) + jnp.log(1.0 + jnp.exp(-jnp.abs(x)))


def _inproj_body(x_ref, nw_ref, w_ref, wg_ref, cw_ref, o_ref, g_ref, gt_ref,
                 h_ref, pad_ref, carry_ref, *, tiles_per_seq, q_tiles, row_blocks):
    i, j = pl.program_id(0), pl.program_id(1)
    tm, tn = o_ref.shape

    @pl.when(j == 0)
    def _():
        h = _rms(x_ref[...], nw_ref[...]).astype(BF16)
        h_ref[...] = h
        g = lax.dot_general(h, wg_ref[0], _NT, preferred_element_type=F32)
        g_ref[...] = g
        gt_ref[...] = g.T[:gt_ref.shape[0]]

    rm = tm // row_blocks

    def block_acc(r):
        return lax.dot_general(h_ref[r * rm:(r + 1) * rm, :], w_ref[0], _NT, preferred_element_type=F32)

    @pl.when(j >= 3 * q_tiles)
    def _():
        for r in range(row_blocks):
            o_ref[r * rm:(r + 1) * rm, :] = block_acc(r).astype(BF16)

    @pl.when(j < 3 * q_tiles)
    def _():
        seq_start = i % tiles_per_seq == 0
        heads = tn // HEAD_DIM

        @pl.when(seq_start)
        def _():
            pad_ref[:, 0:SUBLANES, :] = jnp.zeros((heads, SUBLANES, HEAD_DIM), F32)

        @pl.when(jnp.logical_not(seq_start))
        def _():
            for hh in range(heads):
                pad_ref[hh, 0:SUBLANES, :] = carry_ref[j, :, hh * HEAD_DIM:(hh + 1) * HEAD_DIM]

        is_v = j >= 2 * q_tiles
        scale = jnp.where(j < q_tiles, HEAD_DIM ** -0.5, 1.0)
        base = SUBLANES - (CONV_QKV_TAPS - 1)
        for r in range(row_blocks):
            acc = block_acc(r)
            for hh in range(heads):
                pad_ref[hh, SUBLANES + r * rm:SUBLANES + (r + 1) * rm, :] = acc[:, hh * HEAD_DIM:(hh + 1) * HEAD_DIM]
            if r == row_blocks - 1:
                carry_ref[j] = acc[rm - SUBLANES:]
        for r in range(row_blocks):
            for hh in range(heads):
                cols = slice(hh * HEAD_DIM, (hh + 1) * HEAD_DIM)
                y = cw_ref[0:1, cols] * pad_ref[hh, pl.ds(base + r * rm, rm), :]
                for t in range(1, CONV_QKV_TAPS):
                    y = y + cw_ref[t:t + 1, cols] * pad_ref[hh, pl.ds(base + t + r * rm, rm), :]
                y = _silu(y)
                inv = lax.rsqrt(jnp.sum(y * y, axis=-1, keepdims=True) + EPS) * scale
                o_ref[r * rm:(r + 1) * rm, cols] = (y * jnp.where(is_v, 1.0, inv)).astype(BF16)


def _inproj(x2, norm_w, w_t, conv_w, *, layer, gate_row, n_gate_rows, seq, tm, tn, row_blocks):
    t, d = x2.shape
    ng = n_gate_rows
    nm = w_t.shape[1] - ng
    assert d % tn == 0 and nm % tn == 0 and gate_row % tn == 0 and ng % BF16_SUBLANES == 0
    q_tiles = d // tn
    gate_tile = gate_row // tn
    body = functools.partial(_inproj_body, tiles_per_seq=seq // tm, q_tiles=q_tiles, row_blocks=row_blocks)

    def w_window(i, j):
        row = pl.multiple_of(jnp.where(j < gate_tile, j * tn, j * tn + ng), BF16_SUBLANES)
        return (layer, row, 0)

    element_block = lambda rows: (pl.Element(1), pl.Element(rows), pl.Element(d))
    return pl.pallas_call(
        body,
        grid=(t // tm, nm // tn),
        in_specs=[
            pl.BlockSpec((tm, d), lambda i, j: (i, 0)),
            pl.BlockSpec((None, 1, d), lambda i, j: (layer, 0, 0)),
            pl.BlockSpec(element_block(tn), w_window),
            pl.BlockSpec(element_block(N_GATE_COLS), lambda i, j: (layer, gate_row, 0)),
            pl.BlockSpec((None, CONV_QKV_TAPS, tn), lambda i, j: (layer, 0, jnp.minimum(j, 3 * q_tiles - 1))),
        ],
        out_specs=[
            pl.BlockSpec((tm, tn), lambda i, j: (i, j)),
            pl.BlockSpec((tm, N_GATE_COLS), lambda i, j: (i, 0)),
            pl.BlockSpec((ng, tm), lambda i, j: (0, i)),
        ],
        out_shape=[
            jax.ShapeDtypeStruct((t, nm), BF16),
            jax.ShapeDtypeStruct((t, N_GATE_COLS), F32),
            jax.ShapeDtypeStruct((ng, t), F32),
        ],
        scratch_shapes=[
            pltpu.VMEM((tm, d), BF16),
            pltpu.VMEM((tn // HEAD_DIM, tm + SUBLANES, HEAD_DIM), F32),
            pltpu.VMEM((3 * q_tiles, SUBLANES, tn), F32),
        ],
        compiler_params=_params(("arbitrary", "arbitrary")),
        name="inproj",
    )(x2, norm_w, w_t, w_t, conv_w)


def _segmented_cumsum(x, axis, reverse=False):
    n = x.shape[axis]
    pos = lax.broadcasted_iota(jnp.int32, x.shape, axis) % CHUNK
    shift = 1
    while shift < CHUNK:
        if reverse:
            x = x + jnp.where(pos < CHUNK - shift, pltpu.roll(x, n - shift, axis), 0.0)
        else:
            x = x + jnp.where(pos >= shift, pltpu.roll(x, shift, axis), 0.0)
        shift *= 2
    return x


def _delta_rule_heads(heads, states):
    r, c, nc = GDN_ROWS, CHUNK, CHUNKS_PER_STEP
    row = lax.broadcasted_iota(jnp.int32, (r, r), 0)
    col = lax.broadcasted_iota(jnp.int32, (r, r), 1)
    same_chunk = (row // c) == (col // c)
    causal = same_chunk & (row >= col)
    strict = same_chunk & (row > col)
    eye_cat = (lax.broadcasted_iota(jnp.int32, (c, r), 0)
               == lax.broadcasted_iota(jnp.int32, (c, r), 1) % c).astype(F32)
    chunk_mask = same_chunk.astype(F32).astype(BF16)

    def pack(x_bd):
        out = x_bd[0:c]
        for i in range(1, nc):
            out = out + x_bd[i * c:(i + 1) * c]
        return out

    def unpack(x_cat):
        return jnp.concatenate([x_cat.astype(BF16)] * nc, axis=0) * chunk_mask

    def dot(a, b):
        return jnp.dot(a, b, preferred_element_type=F32)

    qns, kns, attns, n_bds = [], [], [], []
    for qn_b, kn_b, vc, beta_c, gcc, gtail, gcr in heads:
        qn = qn_b.astype(F32)
        kn = kn_b.astype(F32)
        diff = gcc - gcr
        decay = jnp.exp(jnp.where(causal, diff, MASKED_LOG_DECAY))
        decay_strict = jnp.exp(jnp.where(strict, diff, MASKED_LOG_DECAY))
        kq = jnp.concatenate([(kn * (-beta_c)).astype(BF16), qn_b], axis=0)
        gram = lax.dot_general(kq, kn_b, _NT, preferred_element_type=F32)
        qns.append(qn)
        kns.append(kn)
        attns.append((gram[r:] * decay).astype(BF16))
        n_bds.append(gram[:r] * decay_strict)

    n_cats = [pack(n_bd) for n_bd in n_bds]
    ps = [eye_cat + n_cat for n_cat in n_cats]
    w_cats = [dot(n_cat.astype(BF16), n_bd.astype(BF16)) for n_cat, n_bd in zip(n_cats, n_bds)]
    power = 2
    while power < c // 2:
        xs = [dot(jnp.concatenate([p, w_cat], axis=0).astype(BF16), unpack(w_cat))
              for p, w_cat in zip(ps, w_cats)]
        ps = [p + x[:c] for p, x in zip(ps, xs)]
        w_cats = [x[c:] for x in xs]
        power *= 2
    ps = [p + dot(p.astype(BF16), unpack(w_cat)) for p, w_cat in zip(ps, w_cats)]

    uws = []
    for (qn_b, kn_b, vc, beta_c, gcc, gtail, gcr), kn, p in zip(heads, kns, ps):
        rhs = jnp.concatenate([vc * beta_c, kn * (beta_c * jnp.exp(gcc))], axis=1).astype(BF16)
        uws.append(dot(unpack(p), rhs).astype(BF16))
    auws = [dot(attn, uw) for attn, uw in zip(attns, uws)]
    q_effs = [qn * jnp.exp(h[4]) - auw[:, HEAD_DIM:] for qn, h, auw in zip(qns, heads, auws)]
    kes = [(kn * jnp.exp(h[5])).astype(BF16) for kn, h in zip(kns, heads)]

    outs = [[] for _ in heads]
    states = list(states)
    for i in range(nc):
        rows = slice(i * c, (i + 1) * c)
        ebs = [lax.dot_general(ke[rows], uw[rows], _TN, preferred_element_type=F32)
               for ke, uw in zip(kes, uws)]
        pos = [dot(jnp.concatenate([eb[:, HEAD_DIM:], q_eff[rows]], axis=0).astype(BF16), s.astype(BF16))
               for eb, q_eff, s in zip(ebs, q_effs, states)]
        for n, (po, eb, auw, h) in enumerate(zip(pos, ebs, auws, heads)):
            outs[n].append(po[HEAD_DIM:] + auw[rows, :HEAD_DIM])
            dec = jnp.exp(h[6][:, i * c + c - 1:i * c + c])
            states[n] = dec * states[n] - po[:HEAD_DIM] + eb[:, :HEAD_DIM]
    return [jnp.concatenate(o, axis=0) for o in outs], states


def _gdn_body(q_ref, k_ref, v_ref, z_ref, g_ref, gt_ref, al_ref, dt_ref, alr_ref, dtr_ref, nw_ref, o_ref,
              s_ref, grow_ref, *, heads_per_step, n_heads):
    r = GDN_ROWS
    first = pl.program_id(2) == 0

    @pl.when(first)
    def _():
        s_ref[...] = jnp.zeros(s_ref.shape, F32)

    graw = g_ref[...]
    beta_all = jax.nn.sigmoid(graw)
    g_all = -jnp.exp(al_ref[...]) * _softplus(graw + dt_ref[...])
    gc_all = _segmented_cumsum(g_all, 0)
    gtail_all = _segmented_cumsum(g_all, 0, reverse=True) - g_all
    gtraw = gt_ref[...]
    reps = r // N_GATE_COLS
    alr = jnp.concatenate([alr_ref[...]] * reps, axis=1)
    dtr = jnp.concatenate([dtr_ref[...]] * reps, axis=1)
    grow_ref[...] = _segmented_cumsum(-jnp.exp(alr) * _softplus(gtraw + dtr), 1)

    lane = lax.broadcasted_iota(jnp.int32, (r, N_GATE_COLS), 1)
    nw = nw_ref[...]
    heads = []
    for hh in range(heads_per_step):
        head = pl.program_id(1) * heads_per_step + hh
        cols = slice(hh * HEAD_DIM, (hh + 1) * HEAD_DIM)
        beta_c = jnp.sum(jnp.where(lane == head, beta_all, 0.0), axis=1, keepdims=True)
        gcc = jnp.sum(jnp.where(lane == head + n_heads, gc_all, 0.0), axis=1, keepdims=True)
        gtail = jnp.sum(jnp.where(lane == head + n_heads, gtail_all, 0.0), axis=1, keepdims=True)
        gcr = grow_ref[pl.ds(head + n_heads, 1), :]
        heads.append((q_ref[:, cols], k_ref[:, cols], v_ref[:, cols].astype(F32), beta_c, gcc, gtail, gcr))
    outs, states = _delta_rule_heads(heads, [s_ref[hh] for hh in range(heads_per_step)])
    for hh in range(heads_per_step):
        cols = slice(hh * HEAD_DIM, (hh + 1) * HEAD_DIM)
        s_ref[hh] = states[hh]
        o_ref[:, cols] = (_rms(outs[hh], nw) * _silu(z_ref[:, cols].astype(F32))).astype(BF16)


def _gdn(proj, gates, gates_t, al, dt, alr, dtr, norm_w, *, layer, batch, seq, d_lin, heads_per_step):
    t = proj.shape[0]
    n_heads = d_lin // HEAD_DIM
    w = heads_per_step * HEAD_DIM
    nblk = d_lin // w
    r = GDN_ROWS
    steps = seq // r
    row_map = lambda b, g, i: b * steps + i
    ng = gates_t.shape[0]

    def col_spec(offset_blocks):
        return pl.BlockSpec((r, w), lambda b, g, i: (row_map(b, g, i), offset_blocks + g))

    small = lambda shape: pl.BlockSpec((None,) + shape, lambda b, g, i: (layer, 0, 0))
    body = functools.partial(_gdn_body, heads_per_step=heads_per_step, n_heads=n_heads)
    return pl.pallas_call(
        body,
        grid=(batch, nblk, steps),
        in_specs=[
            col_spec(0), col_spec(nblk), col_spec(2 * nblk), col_spec(3 * nblk),
            pl.BlockSpec((r, N_GATE_COLS), lambda b, g, i: (row_map(b, g, i), 0)),
            pl.BlockSpec((ng, r), lambda b, g, i: (0, row_map(b, g, i))),
            small((1, N_GATE_COLS)), small((1, N_GATE_COLS)),
            small((ng, N_GATE_COLS)), small((ng, N_GATE_COLS)),
            small((1, HEAD_DIM)),
        ],
        out_specs=pl.BlockSpec((r, w), lambda b, g, i: (row_map(b, g, i), g)),
        out_shape=jax.ShapeDtypeStruct((t, d_lin), BF16),
        scratch_shapes=[
            pltpu.VMEM((heads_per_step, HEAD_DIM, HEAD_DIM), F32),
            pltpu.VMEM((ng, r), F32),
        ],
        compiler_params=_params(("arbitrary", "arbitrary", "arbitrary")),
        name="gdn",
    )(proj, proj, proj, proj, gates, gates_t, al, dt, alr, dtr, norm_w)


def _mix_body(p_ref, ga_ref, gb_ref, ya_ref, x_ref, pw_ref, ps_ref, wo_ref, o_ref,
              pad_ref, mixed_ref, *, tiles_per_seq):
    tm = x_ref.shape[0]
    gdim = pw_ref.shape[1]
    seq_tile = pl.program_id(0) % tiles_per_seq
    nslab = pad_ref.shape[0]
    per_group = gdim // LANES

    @pl.when(seq_tile == 0)
    def _():
        pad_ref[:, 0:POOL_HALO, :] = jnp.zeros((nslab, POOL_HALO, LANES), F32)

    pos = seq_tile * tm + lax.broadcasted_iota(jnp.int32, (tm, 1), 0)
    for g, win in enumerate(POOL_WINDOWS):
        cols = slice(g * gdim, (g + 1) * gdim)
        cnt = jnp.minimum(pos + 1, win).astype(F32)
        pooled = []
        for c in range(g * per_group, (g + 1) * per_group):
            u = p_ref[:, c * LANES:(c + 1) * LANES].astype(F32)
            pad_ref[c, POOL_HALO:POOL_HALO + tm, :] = u
            acc = u
            for sft in range(1, win):
                acc = acc + pad_ref[c, pl.ds(POOL_HALO - sft, tm), :]
            pad_ref[c, 0:POOL_HALO, :] = pad_ref[c, tm:tm + POOL_HALO, :]
            pooled.append((acc / cnt - u).astype(BF16))
        pooled = jnp.concatenate(pooled, axis=1) if per_group > 1 else pooled[0]
        yb = jnp.dot(pooled, pw_ref[g], preferred_element_type=F32) * ps_ref[:, cols]
        mixed = (jax.nn.sigmoid(ga_ref[:, cols].astype(F32)) * ya_ref[:, cols].astype(F32)
                 + jax.nn.sigmoid(gb_ref[:, cols].astype(F32)) * yb)
        mixed_ref[:, cols] = mixed.astype(BF16)
    o_ref[...] = x_ref[...] + jnp.dot(mixed_ref[...], wo_ref[...], preferred_element_type=F32)


def _mix(proj, ya, x2, pool_w, pool_scale, w_out, *, layer, seq, tm):
    t, d = x2.shape
    _, ngroups, gdim, _ = pool_w.shape
    assert gdim % LANES == 0
    first_blk = proj.shape[1] // d - 3
    row = lambda i: (i, 0)
    const1 = pl.Buffered(1)
    return pl.pallas_call(
        functools.partial(_mix_body, tiles_per_seq=seq // tm),
        grid=(t // tm,),
        in_specs=[
            pl.BlockSpec((tm, d), lambda i: (i, first_blk)),
            pl.BlockSpec((tm, d), lambda i: (i, first_blk + 1)),
            pl.BlockSpec((tm, d), lambda i: (i, first_blk + 2)),
            pl.BlockSpec((tm, d), row),
            pl.BlockSpec((tm, d), row),
            pl.BlockSpec((None, ngroups, gdim, gdim), lambda i: (layer, 0, 0, 0), pipeline_mode=const1),
            pl.BlockSpec((None, 1, d), lambda i: (layer, 0, 0), pipeline_mode=const1),
            pl.BlockSpec((None, d, d), lambda i: (layer, 0, 0), pipeline_mode=const1),
        ],
        out_specs=pl.BlockSpec((tm, d), row),
        out_shape=jax.ShapeDtypeStruct((t, d), F32),
        scratch_shapes=[pltpu.VMEM((d // LANES, tm + POOL_HALO, LANES), F32), pltpu.VMEM((tm, d), BF16)],
        compiler_params=_params(("arbitrary",)),
        name="mix",
    )(proj, proj, proj, ya, x2, pool_w, pool_scale, w_out)


def _ffn_body(x_ref, nw_ref, wg_ref, wu_ref, cw_ref, cb_ref, wd_ref, fw_ref, o_ref,
              h_ref, pad_ref, carry_ref, *, tiles_per_seq, final_norm, row_blocks):
    tm = x_ref.shape[0]
    j = pl.program_id(1)
    seq_start = pl.program_id(0) % tiles_per_seq == 0
    nslab = pad_ref.shape[0]

    @pl.when(j == 0)
    def _():
        x = x_ref[...]
        h_ref[...] = _rms(x, nw_ref[...]).astype(BF16)
        o_ref[...] = x

    @pl.when(seq_start)
    def _():
        pad_ref[:, 0:SUBLANES, :] = jnp.zeros((nslab, SUBLANES, LANES), F32)

    @pl.when(jnp.logical_not(seq_start))
    def _():
        for c in range(nslab):
            pad_ref[c, 0:SUBLANES, :] = carry_ref[j, :, c * LANES:(c + 1) * LANES]

    rm = tm // row_blocks
    base = SUBLANES - (CONV_FFN_TAPS - 1)
    ups = []
    for r in range(row_blocks):
        h = h_ref[r * rm:(r + 1) * rm, :]
        gate = jnp.dot(h, wg_ref[...], preferred_element_type=F32)
        ups.append(jnp.dot(h, wu_ref[...], preferred_element_type=F32))
        for c in range(nslab):
            pad_ref[c, SUBLANES + r * rm:SUBLANES + (r + 1) * rm, :] = gate[:, c * LANES:(c + 1) * LANES]
        if r == row_blocks - 1:
            carry_ref[j] = gate[rm - SUBLANES:]
    for r in range(row_blocks):
        acts = []
        for c in range(nslab):
            cols = slice(c * LANES, (c + 1) * LANES)
            conv = cb_ref[:, cols] + cw_ref[0:1, cols] * pad_ref[c, pl.ds(base + r * rm, rm), :]
            for k in range(1, CONV_FFN_TAPS):
                conv = conv + cw_ref[k:k + 1, cols] * pad_ref[c, pl.ds(base + k + r * rm, rm), :]
            acts.append((0.5 * conv * (1.0 + lax.erf(conv * (2.0 ** -0.5))) * ups[r][:, cols]).astype(BF16))
        act = jnp.concatenate(acts, axis=1)
        o_ref[r * rm:(r + 1) * rm, :] += jnp.dot(act, wd_ref[...], preferred_element_type=F32)

    if final_norm:
        @pl.when(j == pl.num_programs(1) - 1)
        def _():
            o_ref[...] = _rms(o_ref[...], fw_ref[...])


def _ffn(x2, norm_w, w_up, conv_w, conv_b, w_down, final_w, *, layer, final_norm, seq, tm, tf, row_blocks):
    t, d = x2.shape
    dff = w_down.shape[1]
    nf = dff // tf
    return pl.pallas_call(
        functools.partial(_ffn_body, tiles_per_seq=seq // tm, final_norm=final_norm, row_blocks=row_blocks),
        grid=(t // tm, nf),
        in_specs=[
            pl.BlockSpec((tm, d), lambda i, j: (i, 0)),
            pl.BlockSpec((None, 1, d), lambda i, j: (layer, 0, 0)),
            pl.BlockSpec((None, d, tf), lambda i, j: (layer, 0, j)),
            pl.BlockSpec((None, d, tf), lambda i, j: (layer, 0, nf + j)),
            pl.BlockSpec((None, CONV_FFN_TAPS, tf), lambda i, j: (layer, 0, j)),
            pl.BlockSpec((None, 1, tf), lambda i, j: (layer, 0, j)),
            pl.BlockSpec((None, tf, d), lambda i, j: (layer, j, 0)),
            pl.BlockSpec((1, d), lambda i, j: (0, 0)),
        ],
        out_specs=pl.BlockSpec((tm, d), lambda i, j: (i, 0)),
        out_shape=jax.ShapeDtypeStruct((t, d), F32),
        scratch_shapes=[
            pltpu.VMEM((tm, d), BF16),
            pltpu.VMEM((tf // LANES, tm + SUBLANES, LANES), F32),
            pltpu.VMEM((nf, SUBLANES, tf), F32),
        ],
        compiler_params=_params(("arbitrary", "arbitrary")),
        name="ffn",
    )(x2, norm_w, w_up, w_up, conv_w, conv_b, w_down, final_w)


def _tile_plan(seq, d, dff):
    return dict(
        inproj_tm=_pick_tile(seq, 1024), inproj_tn=_pick_tile(d, 2048), inproj_row_blocks=1,
        mix_tm=_pick_tile(seq, 512),
        ffn_tm=_pick_tile(seq, 1024), ffn_tf=_pick_tile(dff, 512), ffn_row_blocks=2,
        heads_per_step=min(16, d // HEAD_DIM),
    )


def kernel(x, norm_mix_w, w_in, conv_qkv_w, a_log, dt_bias, gdn_norm_w, pool_w, pool_scale, w_out,
           norm_ffn_w, w_up, conv_ffn_w, conv_ffn_b, w_down, norm_final_w):
    bsz, seq, d = x.shape
    depth = w_in.shape[0]
    n_heads = d // HEAD_DIM
    dff = w_down.shape[1]
    assert seq % GDN_ROWS == 0 and d % HEAD_DIM == 0 and 2 * n_heads <= N_GATE_COLS
    assert w_in.shape[2] == 7 * d + 2 * n_heads
    plan = _tile_plan(seq, d, dff)
    x2 = x.reshape(bsz * seq, d)
    gate_lo, gate_hi = 4 * d, 4 * d + 2 * n_heads
    lane_pad = N_GATE_COLS - 2 * n_heads

    w_t = jnp.swapaxes(w_in, 1, 2).astype(BF16)
    w_up_b, w_down_b, w_out_b, pool_w_b = (w.astype(BF16) for w in (w_up, w_down, w_out, pool_w))
    row3 = lambda a: a.reshape(depth, 1, a.shape[-1])
    al = row3(jnp.pad(a_log, ((0, 0), (n_heads, lane_pad))))
    dt = row3(jnp.pad(dt_bias, ((0, 0), (n_heads, lane_pad))))
    alr = jnp.broadcast_to(jnp.pad(a_log, ((0, 0), (n_heads, 0)))[:, :, None], (depth, 2 * n_heads, N_GATE_COLS))
    dtr = jnp.broadcast_to(jnp.pad(dt_bias, ((0, 0), (n_heads, 0)))[:, :, None], (depth, 2 * n_heads, N_GATE_COLS))
    norm_mix, norm_ffn, gdn_norm, pool_sc, ffn_b = (row3(a) for a in
                                                    (norm_mix_w, norm_ffn_w, gdn_norm_w, pool_scale, conv_ffn_b))

    for l in range(depth):
        proj, gates, gates_t = _inproj(x2, norm_mix, w_t, conv_qkv_w, layer=l, gate_row=gate_lo,
                                       n_gate_rows=2 * n_heads, seq=seq, tm=plan["inproj_tm"],
                                       tn=plan["inproj_tn"], row_blocks=plan["inproj_row_blocks"])
        ya = _gdn(proj, gates, gates_t, al, dt, alr, dtr, gdn_norm, layer=l,
                  batch=bsz, seq=seq, d_lin=d, heads_per_step=plan["heads_per_step"])
        x2 = _mix(proj, ya, x2, pool_w_b, pool_sc, w_out_b, layer=l, seq=seq, tm=plan["mix_tm"])
        x2 = _ffn(x2, norm_ffn, w_up_b, conv_ffn_w, ffn_b, w_down_b, norm_final_w.reshape(1, d), layer=l,
                  final_norm=(l == depth - 1), seq=seq, tm=plan["ffn_tm"], tf=plan["ffn_tf"],
                  row_blocks=plan["ffn_row_blocks"])
    return x2.reshape(bsz, seq, d)
```

```python
import functools

import jax
import jax.numpy as jnp
from jax import lax
from jax.experimental import pallas as pl
from jax.experimental.pallas import tpu as pltpu

F32 = jnp.float32
BF16 = jnp.bfloat16

HEAD_DIM = 128
LANES = 128
CHUNK = 64
GDN_ROWS = 256
CHUNKS_PER_STEP = GDN_ROWS // CHUNK
CONV_QKV_TAPS = 4
CONV_FFN_TAPS = 3
POOL_WINDOWS = (2, 4, 8, 16)
POOL_HALO = 16
SUBLANES = 8
BF16_SUBLANES = 16
N_GATE_COLS = 128
EPS = 1e-6
MASKED_LOG_DECAY = -1e30
VMEM_LIMIT_BYTES = 60 * 1024 * 1024

_NT = (((1,), (1,)), ((), ()))
_TN = (((0,), (0,)), ((), ()))


def _params(semantics):
    return pltpu.CompilerParams(dimension_semantics=semantics, vmem_limit_bytes=VMEM_LIMIT_BYTES)


def _pick_tile(n, want):
    t = min(n, want)
    while n % t:
        t //= 2
    return t


def _rms(x, w):
    return x * lax.rsqrt(jnp.mean(x * x, axis=-1, keepdims=True) + EPS) * w


def _silu(x):
    return x * jax.nn.sigmoid(x)


def _softplus(x):
    return jnp.maximum(x, 0.0) + jnp.log(1.0 + jnp.exp(-jnp.abs(x)))


def _inproj_body(x_ref, nw_ref, w_ref, wg_ref, cw_ref, o_ref, g_ref, gt_ref,
                 h_ref, pad_ref, carry_ref, *, tiles_per_seq, q_tiles, row_blocks):
    i, j = pl.program_id(0), pl.program_id(1)
    tm, tn = o_ref.shape

    @pl.when(j == 0)
    def _():
        h = _rms(x_ref[...], nw_ref[...]).astype(BF16)
        h_ref[...] = h
        g = lax.dot_general(h, wg_ref[0], _NT, preferred_element_type=F32)
        g_ref[...] = g
        gt_ref[...] = g.T[:gt_ref.shape[0]]

    rm = tm // row_blocks

    def block_acc(r):
        return lax.dot_general(h_ref[r * rm:(r + 1) * rm, :], w_ref[0], _NT, preferred_element_type=F32)

    @pl.when(j >= 3 * q_tiles)
    def _():
        for r in range(row_blocks):
            o_ref[r * rm:(r + 1) * rm, :] = block_acc(r).astype(BF16)

    @pl.when(j < 3 * q_tiles)
    def _():
        seq_start = i % tiles_per_seq == 0
        heads = tn // HEAD_DIM

        @pl.when(seq_start)
        def _():
            pad_ref[:, 0:SUBLANES, :] = jnp.zeros((heads, SUBLANES, HEAD_DIM), F32)

        @pl.when(jnp.logical_not(seq_start))
        def _():
            for hh in range(heads):
                pad_ref[hh, 0:SUBLANES, :] = carry_ref[j, :, hh * HEAD_DIM:(hh + 1) * HEAD_DIM]

        is_v = j >= 2 * q_tiles
        scale = jnp.where(j < q_tiles, HEAD_DIM ** -0.5, 1.0)
        base = SUBLANES - (CONV_QKV_TAPS - 1)
        for r in range(row_blocks):
            acc = block_acc(r)
            for hh in range(heads):
                pad_ref[hh, SUBLANES + r * rm:SUBLANES + (r + 1) * rm, :] = acc[:, hh * HEAD_DIM:(hh + 1) * HEAD_DIM]
            if r == row_blocks - 1:
                carry_ref[j] = acc[rm - SUBLANES:]
        for r in range(row_blocks):
            for hh in range(heads):
                cols = slice(hh * HEAD_DIM, (hh + 1) * HEAD_DIM)
                y = cw_ref[0:1, cols] * pad_ref[hh, pl.ds(base + r * rm, rm), :]
                for t in range(1, CONV_QKV_TAPS):
                    y = y + cw_ref[t:t + 1, cols] * pad_ref[hh, pl.ds(base + t + r * rm, rm), :]
                y = _silu(y)
                inv = lax.rsqrt(jnp.sum(y * y, axis=-1, keepdims=True) + EPS) * scale
                o_ref[r * rm:(r + 1) * rm, cols] = (y * jnp.where(is_v, 1.0, inv)).astype(BF16)


def _inproj(x2, norm_w, w_t, conv_w, *, layer, gate_row, n_gate_rows, seq, tm, tn, row_blocks):
    t, d = x2.shape
    ng = n_gate_rows
    nm = w_t.shape[1] - ng
    assert d % tn == 0 and nm % tn == 0 and gate_row % tn == 0 and ng % BF16_SUBLANES == 0
    q_tiles = d // tn
    gate_tile = gate_row // tn
    body = functools.partial(_inproj_body, tiles_per_seq=seq // tm, q_tiles=q_tiles, row_blocks=row_blocks)

    def w_window(i, j):
        row = pl.multiple_of(jnp.where(j < gate_tile, j * tn, j * tn + ng), BF16_SUBLANES)
        return (layer, row, 0)

    element_block = lambda rows: (pl.Element(1), pl.Element(rows), pl.Element(d))
    return pl.pallas_call(
        body,
        grid=(t // tm, nm // tn),
        in_specs=[
            pl.BlockSpec((tm, d), lambda i, j: (i, 0)),
            pl.BlockSpec((None, 1, d), lambda i, j: (layer, 0, 0)),
            pl.BlockSpec(element_block(tn), w_window),
            pl.BlockSpec(element_block(N_GATE_COLS), lambda i, j: (layer, gate_row, 0)),
            pl.BlockSpec((None, CONV_QKV_TAPS, tn), lambda i, j: (layer, 0, jnp.minimum(j, 3 * q_tiles - 1))),
        ],
        out_specs=[
            pl.BlockSpec((tm, tn), lambda i, j: (i, j)),
            pl.BlockSpec((tm, N_GATE_COLS), lambda i, j: (i, 0)),
            pl.BlockSpec((ng, tm), lambda i, j: (0, i)),
        ],
        out_shape=[
            jax.ShapeDtypeStruct((t, nm), BF16),
            jax.ShapeDtypeStruct((t, N_GATE_COLS), F32),
            jax.ShapeDtypeStruct((ng, t), F32),
        ],
        scratch_shapes=[
            pltpu.VMEM((tm, d), BF16),
            pltpu.VMEM((tn // HEAD_DIM, tm + SUBLANES, HEAD_DIM), F32),
            pltpu.VMEM((3 * q_tiles, SUBLANES, tn), F32),
        ],
        compiler_params=_params(("arbitrary", "arbitrary")),
        name="inproj",
    )(x2, norm_w, w_t, w_t, conv_w)


def _segmented_cumsum(x, axis, reverse=False):
    n = x.shape[axis]
    pos = lax.broadcasted_iota(jnp.int32, x.shape, axis) % CHUNK
    shift = 1
    while shift < CHUNK:
        if reverse:
            x = x + jnp.where(pos < CHUNK - shift, pltpu.roll(x, n - shift, axis), 0.0)
        else:
            x = x + jnp.where(pos >= shift, pltpu.roll(x, shift, axis), 0.0)
        shift *= 2
    return x


def _delta_rule_heads(heads, states):
    r, c, nc = GDN_ROWS, CHUNK, CHUNKS_PER_STEP
    row = lax.broadcasted_iota(jnp.int32, (r, r), 0)
    col = lax.broadcasted_iota(jnp.int32, (r, r), 1)
    same_chunk = (row // c) == (col // c)
    causal = same_chunk & (row >= col)
    strict_f = (row > col).astype(F32)
    eye_cat = (lax.broadcasted_iota(jnp.int32, (c, r), 0)
               == lax.broadcasted_iota(jnp.int32, (c, r), 1) % c).astype(F32)
    chunk_mask = same_chunk.astype(F32).astype(BF16)

    def pack(x_bd):
        out = x_bd[0:c]
        for i in range(1, nc):
            out = out + x_bd[i * c:(i + 1) * c]
        return out

    def unpack(x_cat):
        return jnp.concatenate([x_cat.astype(BF16)] * nc, axis=0) * chunk_mask

    def dot(a, b):
        return jnp.dot(a, b, preferred_element_type=F32)

    qns, kns, attns, n_bds = [], [], [], []
    for qn_b, kn_b, vc, beta_c, gcc, gtail, gcr in heads:
        qn = qn_b.astype(F32)
        kn = kn_b.astype(F32)
        diff = gcc - gcr
        decay = jnp.exp(jnp.where(causal, diff, MASKED_LOG_DECAY))
        decay_strict = decay * strict_f
        kq = jnp.concatenate([(kn * (-beta_c)).astype(BF16), qn_b], axis=0)
        gram = lax.dot_general(kq, kn_b, _NT, preferred_element_type=F32)
        qns.append(qn)
        kns.append(kn)
        attns.append((gram[r:] * decay).astype(BF16))
        n_bds.append(gram[:r] * decay_strict)

    n_cats = [pack(n_bd) for n_bd in n_bds]
    ps = [eye_cat + n_cat for n_cat in n_cats]
    w_cats = [dot(n_cat.astype(BF16), n_bd.astype(BF16)) for n_cat, n_bd in zip(n_cats, n_bds)]
    power = 2
    while power < c // 2:
        xs = [dot(jnp.concatenate([p, w_cat], axis=0).astype(BF16), unpack(w_cat))
              for p, w_cat in zip(ps, w_cats)]
        ps = [p + x[:c] for p, x in zip(ps, xs)]
        w_cats = [x[c:] for x in xs]
        power *= 2
    ps = [p + dot(p.astype(BF16), unpack(w_cat)) for p, w_cat in zip(ps, w_cats)]

    uws = []
    for (qn_b, kn_b, vc, beta_c, gcc, gtail, gcr), kn, p in zip(heads, kns, ps):
        rhs = jnp.concatenate([vc * beta_c, kn * (beta_c * jnp.exp(gcc))], axis=1).astype(BF16)
        uws.append(dot(unpack(p), rhs).astype(BF16))
    auws = [dot(attn, uw) for attn, uw in zip(attns, uws)]
    q_effs = [qn * jnp.exp(h[4]) - auw[:, HEAD_DIM:] for qn, h, auw in zip(qns, heads, auws)]
    kes = [(kn * jnp.exp(h[5])).astype(BF16) for kn, h in zip(kns, heads)]

    outs = [[] for _ in heads]
    states = list(states)
    for i in range(nc):
        rows = slice(i * c, (i + 1) * c)
        ebs = [lax.dot_general(ke[rows], uw[rows], _TN, preferred_element_type=F32)
               for ke, uw in zip(kes, uws)]
        pos = [dot(jnp.concatenate([eb[:, HEAD_DIM:], q_eff[rows]], axis=0).astype(BF16), s.astype(BF16))
               for eb, q_eff, s in zip(ebs, q_effs, states)]
        for n, (po, eb, auw, h) in enumerate(zip(pos, ebs, auws, heads)):
            outs[n].append(po[HEAD_DIM:] + auw[rows, :HEAD_DIM])
            dec = jnp.exp(h[6][:, i * c + c - 1:i * c + c])
            states[n] = dec * states[n] - po[:HEAD_DIM] + eb[:, :HEAD_DIM]
    return [jnp.concatenate(o, axis=0) for o in outs], states


def _gdn_body(q_ref, k_ref, v_ref, z_ref, g_ref, gt_ref, al_ref, dt_ref, alr_ref, dtr_ref, nw_ref, o_ref,
              s_ref, grow_ref, *, heads_per_step, n_heads):
    r = GDN_ROWS
    first = pl.program_id(2) == 0

    @pl.when(first)
    def _():
        s_ref[...] = jnp.zeros(s_ref.shape, F32)

    graw = g_ref[...]
    beta_all = jax.nn.sigmoid(graw)
    g_all = -jnp.exp(al_ref[...]) * _softplus(graw + dt_ref[...])
    gc_all = _segmented_cumsum(g_all, 0)
    gtail_all = _segmented_cumsum(g_all, 0, reverse=True) - g_all
    gtraw = gt_ref[...]
    reps = r // N_GATE_COLS
    alr = jnp.concatenate([alr_ref[...]] * reps, axis=1)
    dtr = jnp.concatenate([dtr_ref[...]] * reps, axis=1)
    grow_ref[...] = _segmented_cumsum(-jnp.exp(alr) * _softplus(gtraw + dtr), 1)

    lane = lax.broadcasted_iota(jnp.int32, (r, N_GATE_COLS), 1)
    nw = nw_ref[...]
    heads = []
    for hh in range(heads_per_step):
        head = pl.program_id(1) * heads_per_step + hh
        cols = slice(hh * HEAD_DIM, (hh + 1) * HEAD_DIM)
        beta_c = jnp.sum(jnp.where(lane == head, beta_all, 0.0), axis=1, keepdims=True)
        gcc = jnp.sum(jnp.where(lane == head + n_heads, gc_all, 0.0), axis=1, keepdims=True)
        gtail = jnp.sum(jnp.where(lane == head + n_heads, gtail_all, 0.0), axis=1, keepdims=True)
        gcr = grow_ref[pl.ds(head + n_heads, 1), :]
        heads.append((q_ref[:, cols], k_ref[:, cols], v_ref[:, cols].astype(F32), beta_c, gcc, gtail, gcr))
    outs, states = _delta_rule_heads(heads, [s_ref[hh] for hh in range(heads_per_step)])
    for hh in range(heads_per_step):
        cols = slice(hh * HEAD_DIM, (hh + 1) * HEAD_DIM)
        s_ref[hh] = states[hh]
        o_ref[:, cols] = (_rms(outs[hh], nw) * _silu(z_ref[:, cols].astype(F32))).astype(BF16)


def _gdn(proj, gates, gates_t, al, dt, alr, dtr, norm_w, *, layer, batch, seq, d_lin, heads_per_step):
    t = proj.shape[0]
    n_heads = d_lin // HEAD_DIM
    w = heads_per_step * HEAD_DIM
    nblk = d_lin // w
    r = GDN_ROWS
    steps = seq // r
    row_map = lambda b, g, i: b * steps + i
    ng = gates_t.shape[0]

    def col_spec(offset_blocks):
        return pl.BlockSpec((r, w), lambda b, g, i: (row_map(b, g, i), offset_blocks + g))

    small = lambda shape: pl.BlockSpec((None,) + shape, lambda b, g, i: (layer, 0, 0))
    body = functools.partial(_gdn_body, heads_per_step=heads_per_step, n_heads=n_heads)
    return pl.pallas_call(
        body,
        grid=(batch, nblk, steps),
        in_specs=[
            col_spec(0), col_spec(nblk), col_spec(2 * nblk), col_spec(3 * nblk),
            pl.BlockSpec((r, N_GATE_COLS), lambda b, g, i: (row_map(b, g, i), 0)),
            pl.BlockSpec((ng, r), lambda b, g, i: (0, row_map(b, g, i))),
            small((1, N_GATE_COLS)), small((1, N_GATE_COLS)),
            small((ng, N_GATE_COLS)), small((ng, N_GATE_COLS)),
            small((1, HEAD_DIM)),
        ],
        out_specs=pl.BlockSpec((r, w), lambda b, g, i: (row_map(b, g, i), g)),
        out_shape=jax.ShapeDtypeStruct((t, d_lin), BF16),
        scratch_shapes=[
            pltpu.VMEM((heads_per_step, HEAD_DIM, HEAD_DIM), F32),
            pltpu.VMEM((ng, r), F32),
        ],
        compiler_params=_params(("arbitrary", "arbitrary", "arbitrary")),
        name="gdn",
    )(proj, proj, proj, proj, gates, gates_t, al, dt, alr, dtr, norm_w)


def _mix_body(p_ref, ga_ref, gb_ref, ya_ref, x_ref, pw_ref, ps_ref, wo_ref, o_ref,
              pad_ref, mixed_ref, *, tiles_per_seq):
    tm = x_ref.shape[0]
    gdim = pw_ref.shape[1]
    seq_tile = pl.program_id(0) % tiles_per_seq
    nslab = pad_ref.shape[0]
    per_group = gdim // LANES

    @pl.when(seq_tile == 0)
    def _():
        pad_ref[:, 0:POOL_HALO, :] = jnp.zeros((nslab, POOL_HALO, LANES), F32)

    pos = seq_tile * tm + lax.broadcasted_iota(jnp.int32, (tm, 1), 0)
    for g, win in enumerate(POOL_WINDOWS):
        cols = slice(g * gdim, (g + 1) * gdim)
        cnt = jnp.minimum(pos + 1, win).astype(F32)
        pooled = []
        for c in range(g * per_group, (g + 1) * per_group):
            u = p_ref[:, c * LANES:(c + 1) * LANES].astype(F32)
            pad_ref[c, POOL_HALO:POOL_HALO + tm, :] = u
            acc = u
            for sft in range(1, win):
                acc = acc + pad_ref[c, pl.ds(POOL_HALO - sft, tm), :]
            pad_ref[c, 0:POOL_HALO, :] = pad_ref[c, tm:tm + POOL_HALO, :]
            pooled.append((acc / cnt - u).astype(BF16))
        pooled = jnp.concatenate(pooled, axis=1) if per_group > 1 else pooled[0]
        yb = jnp.dot(pooled, pw_ref[g], preferred_element_type=F32) * ps_ref[:, cols]
        mixed = (jax.nn.sigmoid(ga_ref[:, cols].astype(F32)) * ya_ref[:, cols].astype(F32)
                 + jax.nn.sigmoid(gb_ref[:, cols].astype(F32)) * yb)
        mixed_ref[:, cols] = mixed.astype(BF16)
    o_ref[...] = x_ref[...] + jnp.dot(mixed_ref[...], wo_ref[...], preferred_element_type=F32)


def _mix(proj, ya, x2, pool_w, pool_scale, w_out, *, layer, seq, tm):
    t, d = x2.shape
    _, ngroups, gdim, _ = pool_w.shape
    assert gdim % LANES == 0
    first_blk = proj.shape[1] // d - 3
    row = lambda i: (i, 0)
    const1 = pl.Buffered(1)
    return pl.pallas_call(
        functools.partial(_mix_body, tiles_per_seq=seq // tm),
        grid=(t // tm,),
        in_specs=[
            pl.BlockSpec((tm, d), lambda i: (i, first_blk)),
            pl.BlockSpec((tm, d), lambda i: (i, first_blk + 1)),
            pl.BlockSpec((tm, d), lambda i: (i, first_blk + 2)),
            pl.BlockSpec((tm, d), row),
            pl.BlockSpec((tm, d), row),
            pl.BlockSpec((None, ngroups, gdim, gdim), lambda i: (layer, 0, 0, 0), pipeline_mode=const1),
            pl.BlockSpec((None, 1, d), lambda i: (layer, 0, 0), pipeline_mode=const1),
            pl.BlockSpec((None, d, d), lambda i: (layer, 0, 0), pipeline_mode=const1),
        ],
        out_specs=pl.BlockSpec((tm, d), row),
        out_shape=jax.ShapeDtypeStruct((t, d), F32),
        scratch_shapes=[pltpu.VMEM((d // LANES, tm + POOL_HALO, LANES), F32), pltpu.VMEM((tm, d), BF16)],
        compiler_params=_params(("arbitrary",)),
        name="mix",
    )(proj, proj, proj, ya, x2, pool_w, pool_scale, w_out)


def _ffn_body(x_ref, nw_ref, wg_ref, wu_ref, cw_ref, cb_ref, wd_ref, fw_ref, o_ref,
              h_ref, pad_ref, carry_ref, *, tiles_per_seq, final_norm, row_blocks):
    tm = x_ref.shape[0]
    j = pl.program_id(1)
    seq_start = pl.program_id(0) % tiles_per_seq == 0
    nslab = pad_ref.shape[0]

    @pl.when(j == 0)
    def _():
        x = x_ref[...]
        h_ref[...] = _rms(x, nw_ref[...]).astype(BF16)
        o_ref[...] = x

    @pl.when(seq_start)
    def _():
        pad_ref[:, 0:SUBLANES, :] = jnp.zeros((nslab, SUBLANES, LANES), F32)

    @pl.when(jnp.logical_not(seq_start))
    def _():
        for c in range(nslab):
            pad_ref[c, 0:SUBLANES, :] = carry_ref[j, :, c * LANES:(c + 1) * LANES]

    rm = tm // row_blocks
    base = SUBLANES - (CONV_FFN_TAPS - 1)
    ups = []
    for r in range(row_blocks):
        h = h_ref[r * rm:(r + 1) * rm, :]
        gate = jnp.dot(h, wg_ref[...], preferred_element_type=F32)
        ups.append(jnp.dot(h, wu_ref[...], preferred_element_type=F32))
        for c in range(nslab):
            pad_ref[c, SUBLANES + r * rm:SUBLANES + (r + 1) * rm, :] = gate[:, c * LANES:(c + 1) * LANES]
        if r == row_blocks - 1:
            carry_ref[j] = gate[rm - SUBLANES:]
    wd = wd_ref[...].astype(BF16)
    for r in range(row_blocks):
        acts = []
        for c in range(nslab):
            cols = slice(c * LANES, (c + 1) * LANES)
            conv = cb_ref[:, cols] + cw_ref[0:1, cols] * pad_ref[c, pl.ds(base + r * rm, rm), :]
            for k in range(1, CONV_FFN_TAPS):
                conv = conv + cw_ref[k:k + 1, cols] * pad_ref[c, pl.ds(base + k + r * rm, rm), :]
            acts.append((0.5 * conv * (1.0 + lax.erf(conv * (2.0 ** -0.5))) * ups[r][:, cols]).astype(BF16))
        act = jnp.concatenate(acts, axis=1)
        o_ref[r * rm:(r + 1) * rm, :] += jnp.dot(act, wd, preferred_element_type=F32)

    if final_norm:
        @pl.when(j == pl.num_programs(1) - 1)
        def _():
            o_ref[...] = _rms(o_ref[...], fw_ref[...])


def _ffn(x2, norm_w, w_up, conv_w, conv_b, w_down, final_w, *, layer, final_norm, seq, tm, tf, row_blocks):
    t, d = x2.shape
    dff = w_down.shape[1]
    nf = dff // tf
    return pl.pallas_call(
        functools.partial(_ffn_body, tiles_per_seq=seq // tm, final_norm=final_norm, row_blocks=row_blocks),
        grid=(t // tm, nf),
        in_specs=[
            pl.BlockSpec((tm, d), lambda i, j: (i, 0)),
            pl.BlockSpec((None, 1, d), lambda i, j: (layer, 0, 0)),
            pl.BlockSpec((None, d, tf), lambda i, j: (layer, 0, j)),
            pl.BlockSpec((None, d, tf), lambda i, j: (layer, 0, nf + j)),
            pl.BlockSpec((None, CONV_FFN_TAPS, tf), lambda i, j: (layer, 0, j)),
            pl.BlockSpec((None, 1, tf), lambda i, j: (layer, 0, j)),
            pl.BlockSpec((None, tf, d), lambda i, j: (layer, j, 0)),
            pl.BlockSpec((1, d), lambda i, j: (0, 0)),
        ],
        out_specs=pl.BlockSpec((tm, d), lambda i, j: (i, 0)),
        out_shape=jax.ShapeDtypeStruct((t, d), F32),
        scratch_shapes=[
            pltpu.VMEM((tm, d), BF16),
            pltpu.VMEM((tf // LANES, tm + SUBLANES, LANES), F32),
            pltpu.VMEM((nf, SUBLANES, tf), F32),
        ],
        compiler_params=_params(("arbitrary", "arbitrary")),
        name="ffn",
    )(x2, norm_w, w_up, w_up, conv_w, conv_b, w_down, final_w)


def _tile_plan(seq, d, dff):
    return dict(
        inproj_tm=_pick_tile(seq, 1024), inproj_tn=_pick_tile(d, 2048), inproj_row_blocks=1,
        mix_tm=_pick_tile(seq, 512),
        ffn_tm=_pick_tile(seq, 1024), ffn_tf=_pick_tile(dff, 512), ffn_row_blocks=2,
        heads_per_step=min(16, d // HEAD_DIM),
    )


def kernel(x, norm_mix_w, w_in, conv_qkv_w, a_log, dt_bias, gdn_norm_w, pool_w, pool_scale, w_out,
           norm_ffn_w, w_up, conv_ffn_w, conv_ffn_b, w_down, norm_final_w):
    bsz, seq, d = x.shape
    depth = w_in.shape[0]
    n_heads = d // HEAD_DIM
    dff = w_down.shape[1]
    assert seq % GDN_ROWS == 0 and d % HEAD_DIM == 0 and 2 * n_heads <= N_GATE_COLS
    assert w_in.shape[2] == 7 * d + 2 * n_heads
    plan = _tile_plan(seq, d, dff)
    x2 = x.reshape(bsz * seq, d)
    gate_lo, gate_hi = 4 * d, 4 * d + 2 * n_heads
    lane_pad = N_GATE_COLS - 2 * n_heads

    w_t = jnp.swapaxes(w_in, 1, 2).astype(BF16)
    w_up_b, w_out_b, pool_w_b = (w.astype(BF16) for w in (w_up, w_out, pool_w))
    row3 = lambda a: a.reshape(depth, 1, a.shape[-1])
    al = row3(jnp.pad(a_log, ((0, 0), (n_heads, lane_pad))))
    dt = row3(jnp.pad(dt_bias, ((0, 0), (n_heads, lane_pad))))
    alr = jnp.broadcast_to(jnp.pad(a_log, ((0, 0), (n_heads, 0)))[:, :, None], (depth, 2 * n_heads, N_GATE_COLS))
    dtr = jnp.broadcast_to(jnp.pad(dt_bias, ((0, 0), (n_heads, 0)))[:, :, None], (depth, 2 * n_heads, N_GATE_COLS))
    norm_mix, norm_ffn, gdn_norm, pool_sc, ffn_b = (row3(a) for a in
                                                    (norm_mix_w, norm_ffn_w, gdn_norm_w, pool_scale, conv_ffn_b))

    for l in range(depth):
        proj, gates, gates_t = _inproj(x2, norm_mix, w_t, conv_qkv_w, layer=l, gate_row=gate_lo,
                                       n_gate_rows=2 * n_heads, seq=seq, tm=plan["inproj_tm"],
                                       tn=plan["inproj_tn"], row_blocks=plan["inproj_row_blocks"])
        ya = _gdn(proj, gates, gates_t, al, dt, alr, dtr, gdn_norm, layer=l,
                  batch=bsz, seq=seq, d_lin=d, heads_per_step=plan["heads_per_step"])
        x2 = _mix(proj, ya, x2, pool_w_b, pool_sc, w_out_b, layer=l, seq=seq, tm=plan["mix_tm"])
        x2 = _ffn(x2, norm_ffn, w_up_b, conv_ffn_w, ffn_b, w_down, norm_final_w.reshape(1, d), layer=l,
                  final_norm=(l == depth - 1), seq=seq, tm=plan["ffn_tm"], tf=plan["ffn_tf"],
                  row_blocks=plan["ffn_row_blocks"])
    return x2.reshape(bsz, seq, d)
```

```python
import functools

import jax
import jax.numpy as jnp
from jax import lax
from jax.experimental import pallas as pl
from jax.experimental.pallas import tpu as pltpu

F32 = jnp.float32
BF16 = jnp.bfloat16

HEAD_DIM = 128
LANES = 128
CHUNK = 64
GDN_ROWS = 256
CHUNKS_PER_STEP = GDN_ROWS // CHUNK
CONV_QKV_TAPS = 4
CONV_FFN_TAPS = 3
POOL_WINDOWS = (2, 4, 8, 16)
POOL_HALO = 16
SUBLANES = 8
BF16_SUBLANES = 16
N_GATE_COLS = 128
EPS = 1e-6
MASKED_LOG_DECAY = -1e30
VMEM_LIMIT_BYTES = 60 * 1024 * 1024

_NT = (((1,), (1,)), ((), ()))
_TN = (((0,), (0,)), ((), ()))


def _params(semantics):
    return pltpu.CompilerParams(dimension_semantics=semantics, vmem_limit_bytes=VMEM_LIMIT_BYTES)


def _pick_tile(n, want):
    t = min(n, want)
    while n % t:
        t //= 2
    return t


def _rms(x, w):
    return x * lax.rsqrt(jnp.mean(x * x, axis=-1, keepdims=True) + EPS) * w


def _silu(x):
    return x * jax.nn.sigmoid(x)


def _softplus(x):
    return jnp.maximum(x, 0.0) + jnp.log(1.0 + jnp.exp(-jnp.abs(x)))


def _inproj_body(x_ref, nw_ref, w_ref, wg_ref, cw_ref, o_ref, g_ref, gt_ref,
                 h_ref, pad_ref, carry_ref, *, tiles_per_seq, q_tiles, row_blocks):
    i, j = pl.program_id(0), pl.program_id(1)
    tm, tn = o_ref.shape

    @pl.when(j == 0)
    def _():
        h = _rms(x_ref[...], nw_ref[...]).astype(BF16)
        h_ref[...] = h
        g = lax.dot_general(h, wg_ref[0], _NT, preferred_element_type=F32)
        g_ref[...] = g
        gt_ref[...] = g.T[:gt_ref.shape[0]]

    rm = tm // row_blocks

    def block_acc(r):
        return lax.dot_general(h_ref[r * rm:(r + 1) * rm, :], w_ref[0], _NT, preferred_element_type=F32)

    @pl.when(j >= 3 * q_tiles)
    def _():
        for r in range(row_blocks):
            o_ref[r * rm:(r + 1) * rm, :] = block_acc(r).astype(BF16)

    @pl.when(j < 3 * q_tiles)
    def _():
        seq_start = i % tiles_per_seq == 0
        heads = tn // HEAD_DIM

        @pl.when(seq_start)
        def _():
            pad_ref[:, 0:SUBLANES, :] = jnp.zeros((heads, SUBLANES, HEAD_DIM), F32)

        @pl.when(jnp.logical_not(seq_start))
        def _():
            for hh in range(heads):
                pad_ref[hh, 0:SUBLANES, :] = carry_ref[j, :, hh * HEAD_DIM:(hh + 1) * HEAD_DIM]

        is_v = j >= 2 * q_tiles
        scale = jnp.where(j < q_tiles, HEAD_DIM ** -0.5, 1.0)
        base = SUBLANES - (CONV_QKV_TAPS - 1)
        for r in range(row_blocks):
            acc = block_acc(r)
            for hh in range(heads):
                pad_ref[hh, SUBLANES + r * rm:SUBLANES + (r + 1) * rm, :] = acc[:, hh * HEAD_DIM:(hh + 1) * HEAD_DIM]
            if r == row_blocks - 1:
                carry_ref[j] = acc[rm - SUBLANES:]
        for r in range(row_blocks):
            for hh in range(heads):
                cols = slice(hh * HEAD_DIM, (hh + 1) * HEAD_DIM)
                y = cw_ref[0:1, cols] * pad_ref[hh, pl.ds(base + r * rm, rm), :]
                for t in range(1, CONV_QKV_TAPS):
                    y = y + cw_ref[t:t + 1, cols] * pad_ref[hh, pl.ds(base + t + r * rm, rm), :]
                y = _silu(y)
                inv = lax.rsqrt(jnp.sum(y * y, axis=-1, keepdims=True) + EPS) * scale
                o_ref[r * rm:(r + 1) * rm, cols] = (y * jnp.where(is_v, 1.0, inv)).astype(BF16)


def _inproj(x2, norm_w, w_t, conv_w, *, layer, gate_row, n_gate_rows, seq, tm, tn, row_blocks):
    t, d = x2.shape
    ng = n_gate_rows
    nm = w_t.shape[1] - ng
    assert d % tn == 0 and nm % tn == 0 and gate_row % tn == 0 and ng % BF16_SUBLANES == 0
    q_tiles = d // tn
    gate_tile = gate_row // tn
    body = functools.partial(_inproj_body, tiles_per_seq=seq // tm, q_tiles=q_tiles, row_blocks=row_blocks)

    def w_window(i, j):
        row = pl.multiple_of(jnp.where(j < gate_tile, j * tn, j * tn + ng), BF16_SUBLANES)
        return (layer, row, 0)

    element_block = lambda rows: (pl.Element(1), pl.Element(rows), pl.Element(d))
    return pl.pallas_call(
        body,
        grid=(t // tm, nm // tn),
        in_specs=[
            pl.BlockSpec((tm, d), lambda i, j: (i, 0)),
            pl.BlockSpec((None, 1, d), lambda i, j: (layer, 0, 0)),
            pl.BlockSpec(element_block(tn), w_window),
            pl.BlockSpec(element_block(N_GATE_COLS), lambda i, j: (layer, gate_row, 0)),
            pl.BlockSpec((None, CONV_QKV_TAPS, tn), lambda i, j: (layer, 0, jnp.minimum(j, 3 * q_tiles - 1))),
        ],
        out_specs=[
            pl.BlockSpec((tm, tn), lambda i, j: (i, j)),
            pl.BlockSpec((tm, N_GATE_COLS), lambda i, j: (i, 0)),
            pl.BlockSpec((ng, tm), lambda i, j: (0, i)),
        ],
        out_shape=[
            jax.ShapeDtypeStruct((t, nm), BF16),
            jax.ShapeDtypeStruct((t, N_GATE_COLS), F32),
            jax.ShapeDtypeStruct((ng, t), F32),
        ],
        scratch_shapes=[
            pltpu.VMEM((tm, d), BF16),
            pltpu.VMEM((tn // HEAD_DIM, tm + SUBLANES, HEAD_DIM), F32),
            pltpu.VMEM((3 * q_tiles, SUBLANES, tn), F32),
        ],
        compiler_params=_params(("arbitrary", "arbitrary")),
        name="inproj",
    )(x2, norm_w, w_t, w_t, conv_w)


def _segmented_cumsum(x, axis, reverse=False):
    n = x.shape[axis]
    pos = lax.broadcasted_iota(jnp.int32, x.shape, axis) % CHUNK
    shift = 1
    while shift < CHUNK:
        if reverse:
            x = x + jnp.where(pos < CHUNK - shift, pltpu.roll(x, n - shift, axis), 0.0)
        else:
            x = x + jnp.where(pos >= shift, pltpu.roll(x, shift, axis), 0.0)
        shift *= 2
    return x


def _delta_rule_heads(heads, states):
    r, c, nc = GDN_ROWS, CHUNK, CHUNKS_PER_STEP
    same_chunk = (lax.broadcasted_iota(jnp.int32, (r, r), 0) // c
                  == lax.broadcasted_iota(jnp.int32, (r, r), 1) // c)
    chunk_mask = same_chunk.astype(F32).astype(BF16)
    row_p = lax.broadcasted_iota(jnp.int32, (c, r), 0)
    col_p = lax.broadcasted_iota(jnp.int32, (c, r), 1)
    lane_chunk = col_p // c
    causal_p = row_p >= col_p % c
    strict_p = (row_p > col_p % c).astype(F32)
    eye_cat = (row_p == col_p % c).astype(F32)

    def unpack(x_cat):
        return jnp.concatenate([x_cat.astype(BF16)] * nc, axis=0) * chunk_mask

    def dot(a, b):
        return jnp.dot(a, b, preferred_element_type=F32)

    qns, kns, attns, n_cats = [], [], [], []
    zeros = jnp.zeros((c, HEAD_DIM), BF16)
    for qn_b, kn_b, vc, beta_c, gcc, gtail, gcr in heads:
        qn = qn_b.astype(F32)
        kn = kn_b.astype(F32)
        kb = (kn * (-beta_c)).astype(BF16)
        lhs = jnp.concatenate(
            [jnp.concatenate([kb[i * c:(i + 1) * c], qn_b[i * c:(i + 1) * c]], axis=0) for i in range(nc)], axis=1)
        rhs_keys = jnp.concatenate(
            [jnp.concatenate([zeros] * i + [kn_b[i * c:(i + 1) * c]] + [zeros] * (nc - 1 - i), axis=1)
             for i in range(nc)], axis=0)
        gram = lax.dot_general(lhs, rhs_keys, _NT, preferred_element_type=F32)
        g_col = jnp.broadcast_to(gcc[(nc - 1) * c:], (c, r))
        for i in range(nc - 2, -1, -1):
            g_col = jnp.where(lane_chunk == i, gcc[i * c:(i + 1) * c], g_col)
        decay = jnp.exp(jnp.where(causal_p, g_col - gcr, MASKED_LOG_DECAY))
        qns.append(qn)
        kns.append(kn)
        attns.append(unpack(gram[c:] * decay))
        n_cats.append(gram[:c] * (decay * strict_p))

    ps = [eye_cat + n_cat for n_cat in n_cats]
    w_cats = [dot(n_cat.astype(BF16), unpack(n_cat)) for n_cat in n_cats]
    power = 2
    while power < c // 2:
        xs = [dot(jnp.concatenate([p, w_cat], axis=0).astype(BF16), unpack(w_cat))
              for p, w_cat in zip(ps, w_cats)]
        ps = [p + x[:c] for p, x in zip(ps, xs)]
        w_cats = [x[c:] for x in xs]
        power *= 2
    ps = [p + dot(p.astype(BF16), unpack(w_cat)) for p, w_cat in zip(ps, w_cats)]

    uws = []
    for (qn_b, kn_b, vc, beta_c, gcc, gtail, gcr), kn, p in zip(heads, kns, ps):
        rhs = jnp.concatenate([vc * beta_c, kn * (beta_c * jnp.exp(gcc))], axis=1).astype(BF16)
        uws.append(dot(unpack(p), rhs).astype(BF16))
    auws = [dot(attn, uw) for attn, uw in zip(attns, uws)]
    q_effs = [qn * jnp.exp(h[4]) - auw[:, HEAD_DIM:] for qn, h, auw in zip(qns, heads, auws)]
    kes = [(kn * jnp.exp(h[5])).astype(BF16) for kn, h in zip(kns, heads)]

    outs = [[] for _ in heads]
    states = list(states)
    for i in range(nc):
        rows = slice(i * c, (i + 1) * c)
        ebs = [lax.dot_general(ke[rows], uw[rows], _TN, preferred_element_type=F32)
               for ke, uw in zip(kes, uws)]
        pos = [dot(jnp.concatenate([eb[:, HEAD_DIM:], q_eff[rows]], axis=0).astype(BF16), s.astype(BF16))
               for eb, q_eff, s in zip(ebs, q_effs, states)]
        for n, (po, eb, auw, h) in enumerate(zip(pos, ebs, auws, heads)):
            outs[n].append(po[HEAD_DIM:] + auw[rows, :HEAD_DIM])
            dec = jnp.exp(h[6][:, i * c + c - 1:i * c + c])
            states[n] = dec * states[n] - po[:HEAD_DIM] + eb[:, :HEAD_DIM]
    return [jnp.concatenate(o, axis=0) for o in outs], states


def _gdn_body(q_ref, k_ref, v_ref, z_ref, g_ref, gt_ref, al_ref, dt_ref, alr_ref, dtr_ref, nw_ref, o_ref,
              s_ref, grow_ref, *, heads_per_step, n_heads):
    r = GDN_ROWS
    first = pl.program_id(2) == 0

    @pl.when(first)
    def _():
        s_ref[...] = jnp.zeros(s_ref.shape, F32)

    graw = g_ref[...]
    beta_all = jax.nn.sigmoid(graw)
    g_all = -jnp.exp(al_ref[...]) * _softplus(graw + dt_ref[...])
    gc_all = _segmented_cumsum(g_all, 0)
    gtail_all = _segmented_cumsum(g_all, 0, reverse=True) - g_all
    gtraw = gt_ref[...]
    reps = r // N_GATE_COLS
    alr = jnp.concatenate([alr_ref[...]] * reps, axis=1)
    dtr = jnp.concatenate([dtr_ref[...]] * reps, axis=1)
    grow_ref[...] = _segmented_cumsum(-jnp.exp(alr) * _softplus(gtraw + dtr), 1)

    lane = lax.broadcasted_iota(jnp.int32, (r, N_GATE_COLS), 1)
    nw = nw_ref[...]
    heads = []
    for hh in range(heads_per_step):
        head = pl.program_id(1) * heads_per_step + hh
        cols = slice(hh * HEAD_DIM, (hh + 1) * HEAD_DIM)
        beta_c = jnp.sum(jnp.where(lane == head, beta_all, 0.0), axis=1, keepdims=True)
        gcc = jnp.sum(jnp.where(lane == head + n_heads, gc_all, 0.0), axis=1, keepdims=True)
        gtail = jnp.sum(jnp.where(lane == head + n_heads, gtail_all, 0.0), axis=1, keepdims=True)
        gcr = grow_ref[pl.ds(head + n_heads, 1), :]
        heads.append((q_ref[:, cols], k_ref[:, cols], v_ref[:, cols].astype(F32), beta_c, gcc, gtail, gcr))
    outs, states = _delta_rule_heads(heads, [s_ref[hh] for hh in range(heads_per_step)])
    for hh in range(heads_per_step):
        cols = slice(hh * HEAD_DIM, (hh + 1) * HEAD_DIM)
        s_ref[hh] = states[hh]
        o_ref[:, cols] = (_rms(outs[hh], nw) * _silu(z_ref[:, cols].astype(F32))).astype(BF16)


def _gdn(proj, gates, gates_t, al, dt, alr, dtr, norm_w, *, layer, batch, seq, d_lin, heads_per_step):
    t = proj.shape[0]
    n_heads = d_lin // HEAD_DIM
    w = heads_per_step * HEAD_DIM
    nblk = d_lin // w
    r = GDN_ROWS
    steps = seq // r
    row_map = lambda b, g, i: b * steps + i
    ng = gates_t.shape[0]

    def col_spec(offset_blocks):
        return pl.BlockSpec((r, w), lambda b, g, i: (row_map(b, g, i), offset_blocks + g))

    small = lambda shape: pl.BlockSpec((None,) + shape, lambda b, g, i: (layer, 0, 0))
    body = functools.partial(_gdn_body, heads_per_step=heads_per_step, n_heads=n_heads)
    return pl.pallas_call(
        body,
        grid=(batch, nblk, steps),
        in_specs=[
            col_spec(0), col_spec(nblk), col_spec(2 * nblk), col_spec(3 * nblk),
            pl.BlockSpec((r, N_GATE_COLS), lambda b, g, i: (row_map(b, g, i), 0)),
            pl.BlockSpec((ng, r), lambda b, g, i: (0, row_map(b, g, i))),
            small((1, N_GATE_COLS)), small((1, N_GATE_COLS)),
            small((ng, N_GATE_COLS)), small((ng, N_GATE_COLS)),
            small((1, HEAD_DIM)),
        ],
        out_specs=pl.BlockSpec((r, w), lambda b, g, i: (row_map(b, g, i), g)),
        out_shape=jax.ShapeDtypeStruct((t, d_lin), BF16),
        scratch_shapes=[
            pltpu.VMEM((heads_per_step, HEAD_DIM, HEAD_DIM), F32),
            pltpu.VMEM((ng, r), F32),
        ],
        compiler_params=_params(("arbitrary", "arbitrary", "arbitrary")),
        name="gdn",
    )(proj, proj, proj, proj, gates, gates_t, al, dt, alr, dtr, norm_w)


def _mix_body(p_ref, ga_ref, gb_ref, ya_ref, x_ref, pw_ref, ps_ref, wo_ref, o_ref,
              pad_ref, mixed_ref, *, tiles_per_seq):
    tm = x_ref.shape[0]
    gdim = pw_ref.shape[1]
    seq_tile = pl.program_id(0) % tiles_per_seq
    nslab = pad_ref.shape[0]
    per_group = gdim // LANES

    @pl.when(seq_tile == 0)
    def _():
        pad_ref[:, 0:POOL_HALO, :] = jnp.zeros((nslab, POOL_HALO, LANES), F32)

    pos = seq_tile * tm + lax.broadcasted_iota(jnp.int32, (tm, 1), 0)
    for g, win in enumerate(POOL_WINDOWS):
        cols = slice(g * gdim, (g + 1) * gdim)
        cnt = jnp.minimum(pos + 1, win).astype(F32)
        pooled = []
        for c in range(g * per_group, (g + 1) * per_group):
            u = p_ref[:, c * LANES:(c + 1) * LANES].astype(F32)
            pad_ref[c, POOL_HALO:POOL_HALO + tm, :] = u
            acc = u
            for sft in range(1, win):
                acc = acc + pad_ref[c, pl.ds(POOL_HALO - sft, tm), :]
            pad_ref[c, 0:POOL_HALO, :] = pad_ref[c, tm:tm + POOL_HALO, :]
            pooled.append((acc / cnt - u).astype(BF16))
        pooled = jnp.concatenate(pooled, axis=1) if per_group > 1 else pooled[0]
        yb = jnp.dot(pooled, pw_ref[g], preferred_element_type=F32) * ps_ref[:, cols]
        mixed = (jax.nn.sigmoid(ga_ref[:, cols].astype(F32)) * ya_ref[:, cols].astype(F32)
                 + jax.nn.sigmoid(gb_ref[:, cols].astype(F32)) * yb)
        mixed_ref[:, cols] = mixed.astype(BF16)
    o_ref[...] = x_ref[...] + jnp.dot(mixed_ref[...], wo_ref[...], preferred_element_type=F32)


def _mix(proj, ya, x2, pool_w, pool_scale, w_out, *, layer, seq, tm):
    t, d = x2.shape
    _, ngroups, gdim, _ = pool_w.shape
    assert gdim % LANES == 0
    first_blk = proj.shape[1] // d - 3
    row = lambda i: (i, 0)
    const1 = pl.Buffered(1)
    return pl.pallas_call(
        functools.partial(_mix_body, tiles_per_seq=seq // tm),
        grid=(t // tm,),
        in_specs=[
            pl.BlockSpec((tm, d), lambda i: (i, first_blk)),
            pl.BlockSpec((tm, d), lambda i: (i, first_blk + 1)),
            pl.BlockSpec((tm, d), lambda i: (i, first_blk + 2)),
            pl.BlockSpec((tm, d), row),
            pl.BlockSpec((tm, d), row),
            pl.BlockSpec((None, ngroups, gdim, gdim), lambda i: (layer, 0, 0, 0), pipeline_mode=const1),
            pl.BlockSpec((None, 1, d), lambda i: (layer, 0, 0), pipeline_mode=const1),
            pl.BlockSpec((None, d, d), lambda i: (layer, 0, 0), pipeline_mode=const1),
        ],
        out_specs=pl.BlockSpec((tm, d), row),
        out_shape=jax.ShapeDtypeStruct((t, d), F32),
        scratch_shapes=[pltpu.VMEM((d // LANES, tm + POOL_HALO, LANES), F32), pltpu.VMEM((tm, d), BF16)],
        compiler_params=_params(("arbitrary",)),
        name="mix",
    )(proj, proj, proj, ya, x2, pool_w, pool_scale, w_out)


def _ffn_body(x_ref, nw_ref, wg_ref, wu_ref, cw_ref, cb_ref, wd_ref, fw_ref, o_ref,
              h_ref, pad_ref, carry_ref, *, tiles_per_seq, final_norm, row_blocks):
    tm = x_ref.shape[0]
    j = pl.program_id(1)
    seq_start = pl.program_id(0) % tiles_per_seq == 0
    nslab = pad_ref.shape[0]

    @pl.when(j == 0)
    def _():
        x = x_ref[...]
        h_ref[...] = _rms(x, nw_ref[...]).astype(BF16)
        o_ref[...] = x

    @pl.when(seq_start)
    def _():
        pad_ref[:, 0:SUBLANES, :] = jnp.zeros((nslab, SUBLANES, LANES), F32)

    @pl.when(jnp.logical_not(seq_start))
    def _():
        for c in range(nslab):
            pad_ref[c, 0:SUBLANES, :] = carry_ref[j, :, c * LANES:(c + 1) * LANES]

    rm = tm // row_blocks
    base = SUBLANES - (CONV_FFN_TAPS - 1)
    ups = []
    for r in range(row_blocks):
        h = h_ref[r * rm:(r + 1) * rm, :]
        gate = jnp.dot(h, wg_ref[...], preferred_element_type=F32)
        ups.append(jnp.dot(h, wu_ref[...], preferred_element_type=F32))
        for c in range(nslab):
            pad_ref[c, SUBLANES + r * rm:SUBLANES + (r + 1) * rm, :] = gate[:, c * LANES:(c + 1) * LANES]
        if r == row_blocks - 1:
            carry_ref[j] = gate[rm - SUBLANES:]
    wd = wd_ref[...].astype(BF16)
    for r in range(row_blocks):
        acts = []
        for c in range(nslab):
            cols = slice(c * LANES, (c + 1) * LANES)
            conv = cb_ref[:, cols] + cw_ref[0:1, cols] * pad_ref[c, pl.ds(base + r * rm, rm), :]
            for k in range(1, CONV_FFN_TAPS):
                conv = conv + cw_ref[k:k + 1, cols] * pad_ref[c, pl.ds(base + k + r * rm, rm), :]
            acts.append((0.5 * conv * (1.0 + lax.erf(conv * (2.0 ** -0.5))) * ups[r][:, cols]).astype(BF16))
        act = jnp.concatenate(acts, axis=1)
        o_ref[r * rm:(r + 1) * rm, :] += jnp.dot(act, wd, preferred_element_type=F32)

    if final_norm:
        @pl.when(j == pl.num_programs(1) - 1)
        def _():
            o_ref[...] = _rms(o_ref[...], fw_ref[...])


def _ffn(x2, norm_w, w_up, conv_w, conv_b, w_down, final_w, *, layer, final_norm, seq, tm, tf, row_blocks):
    t, d = x2.shape
    dff = w_down.shape[1]
    nf = dff // tf
    return pl.pallas_call(
        functools.partial(_ffn_body, tiles_per_seq=seq // tm, final_norm=final_norm, row_blocks=row_blocks),
        grid=(t // tm, nf),
        in_specs=[
            pl.BlockSpec((tm, d), lambda i, j: (i, 0)),
            pl.BlockSpec((None, 1, d), lambda i, j: (layer, 0, 0)),
            pl.BlockSpec((None, d, tf), lambda i, j: (layer, 0, j)),
            pl.BlockSpec((None, d, tf), lambda i, j: (layer, 0, nf + j)),
            pl.BlockSpec((None, CONV_FFN_TAPS, tf), lambda i, j: (layer, 0, j)),
            pl.BlockSpec((None, 1, tf), lambda i, j: (layer, 0, j)),
            pl.BlockSpec((None, tf, d), lambda i, j: (layer, j, 0)),
            pl.BlockSpec((1, d), lambda i, j: (0, 0)),
        ],
        out_specs=pl.BlockSpec((tm, d), lambda i, j: (i, 0)),
        out_shape=jax.ShapeDtypeStruct((t, d), F32),
        scratch_shapes=[
            pltpu.VMEM((tm, d), BF16),
            pltpu.VMEM((tf // LANES, tm + SUBLANES, LANES), F32),
            pltpu.VMEM((nf, SUBLANES, tf), F32),
        ],
        compiler_params=_params(("arbitrary", "arbitrary")),
        name="ffn",
    )(x2, norm_w, w_up, w_up, conv_w, conv_b, w_down, final_w)


def _tile_plan(seq, d, dff):
    return dict(
        inproj_tm=_pick_tile(seq, 1024), inproj_tn=_pick_tile(d, 2048), inproj_row_blocks=1,
        mix_tm=_pick_tile(seq, 512),
        ffn_tm=_pick_tile(seq, 1024), ffn_tf=_pick_tile(dff, 512), ffn_row_blocks=2,
        heads_per_step=min(16, d // HEAD_DIM),
    )


def kernel(x, norm_mix_w, w_in, conv_qkv_w, a_log, dt_bias, gdn_norm_w, pool_w, pool_scale, w_out,
           norm_ffn_w, w_up, conv_ffn_w, conv_ffn_b, w_down, norm_final_w):
    bsz, seq, d = x.shape
    depth = w_in.shape[0]
    n_heads = d // HEAD_DIM
    dff = w_down.shape[1]
    assert seq % GDN_ROWS == 0 and d % HEAD_DIM == 0 and 2 * n_heads <= N_GATE_COLS
    assert w_in.shape[2] == 7 * d + 2 * n_heads
    plan = _tile_plan(seq, d, dff)
    x2 = x.reshape(bsz * seq, d)
    gate_lo = 4 * d
    lane_pad = N_GATE_COLS - 2 * n_heads

    w_t = jnp.swapaxes(w_in, 1, 2).astype(BF16)
    w_up_b, w_out_b, pool_w_b = (w.astype(BF16) for w in (w_up, w_out, pool_w))
    row3 = lambda a: a.reshape(depth, 1, a.shape[-1])
    al = row3(jnp.pad(a_log, ((0, 0), (n_heads, lane_pad))))
    dt = row3(jnp.pad(dt_bias, ((0, 0), (n_heads, lane_pad))))
    alr = jnp.broadcast_to(jnp.pad(a_log, ((0, 0), (n_heads, 0)))[:, :, None], (depth, 2 * n_heads, N_GATE_COLS))
    dtr = jnp.broadcast_to(jnp.pad(dt_bias, ((0, 0), (n_heads, 0)))[:, :, None], (depth, 2 * n_heads, N_GATE_COLS))
    norm_mix, norm_ffn, gdn_norm, pool_sc, ffn_b = (row3(a) for a in
                                                    (norm_mix_w, norm_ffn_w, gdn_norm_w, pool_scale, conv_ffn_b))

    for l in range(depth):
        proj, gates, gates_t = _inproj(x2, norm_mix, w_t, conv_qkv_w, layer=l, gate_row=gate_lo,
                                       n_gate_rows=2 * n_heads, seq=seq, tm=plan["inproj_tm"],
                                       tn=plan["inproj_tn"], row_blocks=plan["inproj_row_blocks"])
        ya = _gdn(proj, gates, gates_t, al, dt, alr, dtr, gdn_norm, layer=l,
                  batch=bsz, seq=seq, d_lin=d, heads_per_step=plan["heads_per_step"])
        x2 = _mix(proj, ya, x2, pool_w_b, pool_sc, w_out_b, layer=l, seq=seq, tm=plan["mix_tm"])
        x2 = _ffn(x2, norm_ffn, w_up_b, conv_ffn_w, ffn_b, w_down, norm_final_w.reshape(1, d), layer=l,
                  final_norm=(l == depth - 1), seq=seq, tm=plan["ffn_tm"], tf=plan["ffn_tf"],
                  row_blocks=plan["ffn_row_blocks"])
    return x2.reshape(bsz, seq, d)
```

```python
import functools

import jax
import jax.numpy as jnp
from jax import lax
from jax.experimental import pallas as pl
from jax.experimental.pallas import tpu as pltpu

F32 = jnp.float32
BF16 = jnp.bfloat16

HEAD_DIM = 128
LANES = 128
CHUNK = 64
GDN_ROWS = 256
CHUNKS_PER_STEP = GDN_ROWS // CHUNK
CONV_QKV_TAPS = 4
CONV_FFN_TAPS = 3
POOL_WINDOWS = (2, 4, 8, 16)
POOL_HALO = 16
SUBLANES = 8
BF16_SUBLANES = 16
N_GATE_COLS = 128
EPS = 1e-6
MASKED_LOG_DECAY = -1e30
VMEM_LIMIT_BYTES = 60 * 1024 * 1024

_NT = (((1,), (1,)), ((), ()))
_TN = (((0,), (0,)), ((), ()))


def _params(semantics):
    return pltpu.CompilerParams(dimension_semantics=semantics, vmem_limit_bytes=VMEM_LIMIT_BYTES)


def _pick_tile(n, want):
    t = min(n, want)
    while n % t:
        t //= 2
    return t


def _rms(x, w):
    return x * lax.rsqrt(jnp.mean(x * x, axis=-1, keepdims=True) + EPS) * w


def _silu(x):
    return x * jax.nn.sigmoid(x)


def _softplus(x):
    return jnp.maximum(x, 0.0) + jnp.log(1.0 + jnp.exp(-jnp.abs(x)))


def _inproj_body(x_ref, nw_ref, w_ref, wg_ref, cw_ref, o_ref, g_ref, gt_ref,
                 h_ref, pad_ref, carry_ref, *, tiles_per_seq, q_tiles, row_blocks):
    i, j = pl.program_id(0), pl.program_id(1)
    tm, tn = o_ref.shape

    @pl.when(j == 0)
    def _():
        h = _rms(x_ref[...], nw_ref[...]).astype(BF16)
        h_ref[...] = h
        g = lax.dot_general(h, wg_ref[0], _NT, preferred_element_type=F32)
        g_ref[...] = g
        gt_ref[...] = g.T[:gt_ref.shape[0]]

    rm = tm // row_blocks

    def block_acc(r):
        return lax.dot_general(h_ref[r * rm:(r + 1) * rm, :], w_ref[0], _NT, preferred_element_type=F32)

    @pl.when(j >= 3 * q_tiles)
    def _():
        for r in range(row_blocks):
            o_ref[r * rm:(r + 1) * rm, :] = block_acc(r).astype(BF16)

    @pl.when(j < 3 * q_tiles)
    def _():
        seq_start = i % tiles_per_seq == 0
        heads = tn // HEAD_DIM

        @pl.when(seq_start)
        def _():
            pad_ref[:, 0:SUBLANES, :] = jnp.zeros((heads, SUBLANES, HEAD_DIM), F32)

        @pl.when(jnp.logical_not(seq_start))
        def _():
            for hh in range(heads):
                pad_ref[hh, 0:SUBLANES, :] = carry_ref[j, :, hh * HEAD_DIM:(hh + 1) * HEAD_DIM]

        is_v = j >= 2 * q_tiles
        scale = jnp.where(j < q_tiles, HEAD_DIM ** -0.5, 1.0)
        base = SUBLANES - (CONV_QKV_TAPS - 1)
        for r in range(row_blocks):
            acc = block_acc(r)
            for hh in range(heads):
                pad_ref[hh, SUBLANES + r * rm:SUBLANES + (r + 1) * rm, :] = acc[:, hh * HEAD_DIM:(hh + 1) * HEAD_DIM]
            if r == row_blocks - 1:
                carry_ref[j] = acc[rm - SUBLANES:]
        for r in range(row_blocks):
            for hh in range(heads):
                cols = slice(hh * HEAD_DIM, (hh + 1) * HEAD_DIM)
                y = cw_ref[0:1, cols] * pad_ref[hh, pl.ds(base + r * rm, rm), :]
                for t in range(1, CONV_QKV_TAPS):
                    y = y + cw_ref[t:t + 1, cols] * pad_ref[hh, pl.ds(base + t + r * rm, rm), :]
                y = _silu(y)
                inv = lax.rsqrt(jnp.sum(y * y, axis=-1, keepdims=True) + EPS) * scale
                o_ref[r * rm:(r + 1) * rm, cols] = (y * jnp.where(is_v, 1.0, inv)).astype(BF16)


def _inproj(x2, norm_w, w_t, conv_w, *, layer, gate_row, n_gate_rows, seq, tm, tn, row_blocks):
    t, d = x2.shape
    ng = n_gate_rows
    nm = w_t.shape[1] - ng
    assert d % tn == 0 and nm % tn == 0 and gate_row % tn == 0 and ng % BF16_SUBLANES == 0
    q_tiles = d // tn
    gate_tile = gate_row // tn
    body = functools.partial(_inproj_body, tiles_per_seq=seq // tm, q_tiles=q_tiles, row_blocks=row_blocks)

    def w_window(i, j):
        row = pl.multiple_of(jnp.where(j < gate_tile, j * tn, j * tn + ng), BF16_SUBLANES)
        return (layer, row, 0)

    element_block = lambda rows: (pl.Element(1), pl.Element(rows), pl.Element(d))
    return pl.pallas_call(
        body,
        grid=(t // tm, nm // tn),
        in_specs=[
            pl.BlockSpec((tm, d), lambda i, j: (i, 0)),
            pl.BlockSpec((None, 1, d), lambda i, j: (layer, 0, 0)),
            pl.BlockSpec(element_block(tn), w_window),
            pl.BlockSpec(element_block(N_GATE_COLS), lambda i, j: (layer, gate_row, 0)),
            pl.BlockSpec((None, CONV_QKV_TAPS, tn), lambda i, j: (layer, 0, jnp.minimum(j, 3 * q_tiles - 1))),
        ],
        out_specs=[
            pl.BlockSpec((tm, tn), lambda i, j: (i, j)),
            pl.BlockSpec((tm, N_GATE_COLS), lambda i, j: (i, 0)),
            pl.BlockSpec((ng, tm), lambda i, j: (0, i)),
        ],
        out_shape=[
            jax.ShapeDtypeStruct((t, nm), BF16),
            jax.ShapeDtypeStruct((t, N_GATE_COLS), F32),
            jax.ShapeDtypeStruct((ng, t), F32),
        ],
        scratch_shapes=[
            pltpu.VMEM((tm, d), BF16),
            pltpu.VMEM((tn // HEAD_DIM, tm + SUBLANES, HEAD_DIM), F32),
            pltpu.VMEM((3 * q_tiles, SUBLANES, tn), F32),
        ],
        compiler_params=_params(("arbitrary", "arbitrary")),
        name="inproj",
    )(x2, norm_w, w_t, w_t, conv_w)


def _segmented_cumsum(x, axis, reverse=False):
    n = x.shape[axis]
    pos = lax.broadcasted_iota(jnp.int32, x.shape, axis) % CHUNK
    shift = 1
    while shift < CHUNK:
        if reverse:
            x = x + jnp.where(pos < CHUNK - shift, pltpu.roll(x, n - shift, axis), 0.0)
        else:
            x = x + jnp.where(pos >= shift, pltpu.roll(x, shift, axis), 0.0)
        shift *= 2
    return x


def _delta_rule_heads(heads, states):
    r, c, nc = GDN_ROWS, CHUNK, CHUNKS_PER_STEP
    same_chunk = (lax.broadcasted_iota(jnp.int32, (r, r), 0) // c
                  == lax.broadcasted_iota(jnp.int32, (r, r), 1) // c)
    chunk_mask = same_chunk.astype(F32).astype(BF16)
    row_p = lax.broadcasted_iota(jnp.int32, (c, r), 0)
    col_p = lax.broadcasted_iota(jnp.int32, (c, r), 1)
    lane_chunk = col_p // c
    causal_p = row_p >= col_p % c
    strict_p = (row_p > col_p % c).astype(F32)
    eye_cat = (row_p == col_p % c).astype(F32)

    def unpack(x_cat):
        return jnp.concatenate([x_cat.astype(BF16)] * nc, axis=0) * chunk_mask

    def dot(a, b):
        return jnp.dot(a, b, preferred_element_type=F32)

    qns, kns, attns, n_cats = [], [], [], []
    zeros = jnp.zeros((c, HEAD_DIM), BF16)
    for qn_b, kn_b, vc, beta_c, gcc, gtail, gcr in heads:
        qn = qn_b.astype(F32)
        kn = kn_b.astype(F32)
        kb = (kn * (-beta_c)).astype(BF16)
        lhs = jnp.concatenate(
            [jnp.concatenate([kb[i * c:(i + 1) * c], qn_b[i * c:(i + 1) * c]], axis=0) for i in range(nc)], axis=1)
        rhs_keys = jnp.concatenate(
            [jnp.concatenate([zeros] * i + [kn_b[i * c:(i + 1) * c]] + [zeros] * (nc - 1 - i), axis=1)
             for i in range(nc)], axis=0)
        gram = lax.dot_general(lhs, rhs_keys, _NT, preferred_element_type=F32)
        g_col = jnp.broadcast_to(gcc[(nc - 1) * c:], (c, r))
        for i in range(nc - 2, -1, -1):
            g_col = jnp.where(lane_chunk == i, gcc[i * c:(i + 1) * c], g_col)
        decay = jnp.exp(jnp.where(causal_p, g_col - gcr, MASKED_LOG_DECAY))
        qns.append(qn)
        kns.append(kn)
        attns.append(unpack(gram[c:] * decay))
        n_cats.append(gram[:c] * (decay * strict_p))

    ps = [eye_cat + n_cat for n_cat in n_cats]
    w_cats = [dot(n_cat.astype(BF16), unpack(n_cat)) for n_cat in n_cats]
    power = 2
    while power < c // 2:
        xs = [dot(jnp.concatenate([p, w_cat], axis=0).astype(BF16), unpack(w_cat))
              for p, w_cat in zip(ps, w_cats)]
        ps = [p + x[:c] for p, x in zip(ps, xs)]
        w_cats = [x[c:] for x in xs]
        power *= 2
    ps = [p + dot(p.astype(BF16), unpack(w_cat)) for p, w_cat in zip(ps, w_cats)]

    uws = []
    for (qn_b, kn_b, vc, beta_c, gcc, gtail, gcr), kn, p in zip(heads, kns, ps):
        rhs = jnp.concatenate([vc * beta_c, kn * (beta_c * jnp.exp(gcc))], axis=1).astype(BF16)
        uws.append(dot(unpack(p), rhs).astype(BF16))
    auws = [dot(attn, uw) for attn, uw in zip(attns, uws)]
    q_effs = [qn * jnp.exp(h[4]) - auw[:, HEAD_DIM:] for qn, h, auw in zip(qns, heads, auws)]
    kes = [(kn * jnp.exp(h[5])).astype(BF16) for kn, h in zip(kns, heads)]

    outs = [[] for _ in heads]
    states = list(states)
    for i in range(nc):
        rows = slice(i * c, (i + 1) * c)
        ebs = [lax.dot_general(ke[rows], uw[rows], _TN, preferred_element_type=F32)
               for ke, uw in zip(kes, uws)]
        pos = [dot(jnp.concatenate([eb[:, HEAD_DIM:], q_eff[rows]], axis=0).astype(BF16), s.astype(BF16))
               for eb, q_eff, s in zip(ebs, q_effs, states)]
        for n, (po, eb, auw, h) in enumerate(zip(pos, ebs, auws, heads)):
            outs[n].append(po[HEAD_DIM:] + auw[rows, :HEAD_DIM])
            dec = jnp.exp(h[6][:, i * c + c - 1:i * c + c])
            states[n] = dec * states[n] - po[:HEAD_DIM] + eb[:, :HEAD_DIM]
    return [jnp.concatenate(o, axis=0) for o in outs], states


def _gdn_body(q_ref, k_ref, v_ref, z_ref, g_ref, gt_ref, al_ref, dt_ref, alr_ref, dtr_ref, nw_ref, o_ref,
              s_ref, grow_ref, *, heads_per_step, n_heads):
    r = GDN_ROWS
    first = pl.program_id(2) == 0

    @pl.when(first)
    def _():
        s_ref[...] = jnp.zeros(s_ref.shape, F32)

    graw = g_ref[...]
    beta_all = jax.nn.sigmoid(graw)
    g_all = -jnp.exp(al_ref[...]) * _softplus(graw + dt_ref[...])
    gc_all = _segmented_cumsum(g_all, 0)
    gtail_all = _segmented_cumsum(g_all, 0, reverse=True) - g_all
    gtraw = gt_ref[...]
    reps = r // N_GATE_COLS
    alr = jnp.concatenate([alr_ref[...]] * reps, axis=1)
    dtr = jnp.concatenate([dtr_ref[...]] * reps, axis=1)
    grow_ref[...] = _segmented_cumsum(-jnp.exp(alr) * _softplus(gtraw + dtr), 1)

    lane = lax.broadcasted_iota(jnp.int32, (r, N_GATE_COLS), 1)
    nw = nw_ref[...]
    heads = []
    for hh in range(heads_per_step):
        head = pl.program_id(1) * heads_per_step + hh
        cols = slice(hh * HEAD_DIM, (hh + 1) * HEAD_DIM)
        beta_c = jnp.sum(jnp.where(lane == head, beta_all, 0.0), axis=1, keepdims=True)
        gcc = jnp.sum(jnp.where(lane == head + n_heads, gc_all, 0.0), axis=1, keepdims=True)
        gtail = jnp.sum(jnp.where(lane == head + n_heads, gtail_all, 0.0), axis=1, keepdims=True)
        gcr = grow_ref[pl.ds(head + n_heads, 1), :]
        heads.append((q_ref[:, cols], k_ref[:, cols], v_ref[:, cols].astype(F32), beta_c, gcc, gtail, gcr))
    outs, states = _delta_rule_heads(heads, [s_ref[hh] for hh in range(heads_per_step)])
    for hh in range(heads_per_step):
        cols = slice(hh * HEAD_DIM, (hh + 1) * HEAD_DIM)
        s_ref[hh] = states[hh]
        o_ref[:, cols] = (_rms(outs[hh], nw) * _silu(z_ref[:, cols].astype(F32))).astype(BF16)


def _gdn(proj, gates, gates_t, al, dt, alr, dtr, norm_w, *, layer, batch, seq, d_lin, heads_per_step):
    t = proj.shape[0]
    n_heads = d_lin // HEAD_DIM
    w = heads_per_step * HEAD_DIM
    nblk = d_lin // w
    r = GDN_ROWS
    steps = seq // r
    row_map = lambda b, g, i: b * steps + i
    ng = gates_t.shape[0]

    def col_spec(offset_blocks):
        return pl.BlockSpec((r, w), lambda b, g, i: (row_map(b, g, i), offset_blocks + g))

    small = lambda shape: pl.BlockSpec((None,) + shape, lambda b, g, i: (layer, 0, 0))
    body = functools.partial(_gdn_body, heads_per_step=heads_per_step, n_heads=n_heads)
    return pl.pallas_call(
        body,
        grid=(batch, nblk, steps),
        in_specs=[
            col_spec(0), col_spec(nblk), col_spec(2 * nblk), col_spec(3 * nblk),
            pl.BlockSpec((r, N_GATE_COLS), lambda b, g, i: (row_map(b, g, i), 0)),
            pl.BlockSpec((ng, r), lambda b, g, i: (0, row_map(b, g, i))),
            small((1, N_GATE_COLS)), small((1, N_GATE_COLS)),
            small((ng, N_GATE_COLS)), small((ng, N_GATE_COLS)),
            small((1, HEAD_DIM)),
        ],
        out_specs=pl.BlockSpec((r, w), lambda b, g, i: (row_map(b, g, i), g)),
        out_shape=jax.ShapeDtypeStruct((t, d_lin), BF16),
        scratch_shapes=[
            pltpu.VMEM((heads_per_step, HEAD_DIM, HEAD_DIM), F32),
            pltpu.VMEM((ng, r), F32),
        ],
        compiler_params=_params(("arbitrary", "arbitrary", "arbitrary")),
        name="gdn",
    )(proj, proj, proj, proj, gates, gates_t, al, dt, alr, dtr, norm_w)


def _mix_body(p_ref, ga_ref, gb_ref, ya_ref, x_ref, pw_ref, ps_ref, wo_ref, o_ref,
              pad_ref, mixed_ref, *, tiles_per_seq):
    tm = x_ref.shape[0]
    gdim = pw_ref.shape[1]
    seq_tile = pl.program_id(0) % tiles_per_seq
    nslab = pad_ref.shape[0]
    per_group = gdim // LANES

    @pl.when(seq_tile == 0)
    def _():
        pad_ref[:, 0:POOL_HALO, :] = jnp.zeros((nslab, POOL_HALO, LANES), F32)

    pos = seq_tile * tm + lax.broadcasted_iota(jnp.int32, (tm, 1), 0)
    for g, win in enumerate(POOL_WINDOWS):
        cols = slice(g * gdim, (g + 1) * gdim)
        cnt = jnp.minimum(pos + 1, win).astype(F32)
        pooled = []
        for c in range(g * per_group, (g + 1) * per_group):
            u = p_ref[:, c * LANES:(c + 1) * LANES].astype(F32)
            pad_ref[c, POOL_HALO:POOL_HALO + tm, :] = u
            acc = u
            for sft in range(1, win):
                acc = acc + pad_ref[c, pl.ds(POOL_HALO - sft, tm), :]
            pad_ref[c, 0:POOL_HALO, :] = pad_ref[c, tm:tm + POOL_HALO, :]
            pooled.append((acc / cnt - u).astype(BF16))
        pooled = jnp.concatenate(pooled, axis=1) if per_group > 1 else pooled[0]
        yb = jnp.dot(pooled, pw_ref[g], preferred_element_type=F32) * ps_ref[:, cols]
        mixed = (jax.nn.sigmoid(ga_ref[:, cols].astype(F32)) * ya_ref[:, cols].astype(F32)
                 + jax.nn.sigmoid(gb_ref[:, cols].astype(F32)) * yb)
        mixed_ref[:, cols] = mixed.astype(BF16)
    o_ref[...] = x_ref[...] + jnp.dot(mixed_ref[...], wo_ref[...], preferred_element_type=F32)


def _mix(proj, ya, x2, pool_w, pool_scale, w_out, *, layer, seq, tm):
    t, d = x2.shape
    _, ngroups, gdim, _ = pool_w.shape
    assert gdim % LANES == 0
    first_blk = proj.shape[1] // d - 3
    row = lambda i: (i, 0)
    const1 = pl.Buffered(1)
    return pl.pallas_call(
        functools.partial(_mix_body, tiles_per_seq=seq // tm),
        grid=(t // tm,),
        in_specs=[
            pl.BlockSpec((tm, d), lambda i: (i, first_blk)),
            pl.BlockSpec((tm, d), lambda i: (i, first_blk + 1)),
            pl.BlockSpec((tm, d), lambda i: (i, first_blk + 2)),
            pl.BlockSpec((tm, d), row),
            pl.BlockSpec((tm, d), row),
            pl.BlockSpec((None, ngroups, gdim, gdim), lambda i: (layer, 0, 0, 0), pipeline_mode=const1),
            pl.BlockSpec((None, 1, d), lambda i: (layer, 0, 0), pipeline_mode=const1),
            pl.BlockSpec((None, d, d), lambda i: (layer, 0, 0), pipeline_mode=const1),
        ],
        out_specs=pl.BlockSpec((tm, d), row),
        out_shape=jax.ShapeDtypeStruct((t, d), F32),
        scratch_shapes=[pltpu.VMEM((d // LANES, tm + POOL_HALO, LANES), F32), pltpu.VMEM((tm, d), BF16)],
        compiler_params=_params(("arbitrary",)),
        name="mix",
    )(proj, proj, proj, ya, x2, pool_w, pool_scale, w_out)


def _ffn_body(x_ref, nw_ref, wg_ref, wu_ref, cw_ref, cb_ref, wd_ref, fw_ref, o_ref,
              h_ref, pad_ref, carry_ref, *, tiles_per_seq, final_norm, row_blocks):
    tm = x_ref.shape[0]
    j = pl.program_id(1)
    seq_start = pl.program_id(0) % tiles_per_seq == 0
    nslab = pad_ref.shape[0]

    @pl.when(j == 0)
    def _():
        h_ref[...] = _rms(x_ref[...], nw_ref[...]).astype(BF16)

    @pl.when(seq_start)
    def _():
        pad_ref[:, 0:SUBLANES, :] = jnp.zeros((nslab, SUBLANES, LANES), F32)

    @pl.when(jnp.logical_not(seq_start))
    def _():
        for c in range(nslab):
            pad_ref[c, 0:SUBLANES, :] = carry_ref[j, :, c * LANES:(c + 1) * LANES]

    def step(first):
        rm = tm // row_blocks
        base = SUBLANES - (CONV_FFN_TAPS - 1)
        ups = []
        for r in range(row_blocks):
            h = h_ref[r * rm:(r + 1) * rm, :]
            gate = jnp.dot(h, wg_ref[...], preferred_element_type=F32)
            ups.append(jnp.dot(h, wu_ref[...], preferred_element_type=F32))
            for c in range(nslab):
                pad_ref[c, SUBLANES + r * rm:SUBLANES + (r + 1) * rm, :] = gate[:, c * LANES:(c + 1) * LANES]
            if r == row_blocks - 1:
                carry_ref[j] = gate[rm - SUBLANES:]
        wd = wd_ref[...].astype(BF16)
        for r in range(row_blocks):
            acts = []
            for c in range(nslab):
                cols = slice(c * LANES, (c + 1) * LANES)
                conv = cb_ref[:, cols] + cw_ref[0:1, cols] * pad_ref[c, pl.ds(base + r * rm, rm), :]
                for k in range(1, CONV_FFN_TAPS):
                    conv = conv + cw_ref[k:k + 1, cols] * pad_ref[c, pl.ds(base + k + r * rm, rm), :]
                acts.append((0.5 * conv * (1.0 + lax.erf(conv * (2.0 ** -0.5))) * ups[r][:, cols]).astype(BF16))
            act = jnp.concatenate(acts, axis=1)
            down = jnp.dot(act, wd, preferred_element_type=F32)
            if first:
                o_ref[r * rm:(r + 1) * rm, :] = x_ref[r * rm:(r + 1) * rm, :] + down
            else:
                o_ref[r * rm:(r + 1) * rm, :] += down

    pl.when(j == 0)(functools.partial(step, True))
    pl.when(j > 0)(functools.partial(step, False))

    if final_norm:
        @pl.when(j == pl.num_programs(1) - 1)
        def _():
            o_ref[...] = _rms(o_ref[...], fw_ref[...])


def _ffn(x2, norm_w, w_up, conv_w, conv_b, w_down, final_w, *, layer, final_norm, seq, tm, tf, row_blocks):
    t, d = x2.shape
    dff = w_down.shape[1]
    nf = dff // tf
    return pl.pallas_call(
        functools.partial(_ffn_body, tiles_per_seq=seq // tm, final_norm=final_norm, row_blocks=row_blocks),
        grid=(t // tm, nf),
        in_specs=[
            pl.BlockSpec((tm, d), lambda i, j: (i, 0)),
            pl.BlockSpec((None, 1, d), lambda i, j: (layer, 0, 0)),
            pl.BlockSpec((None, d, tf), lambda i, j: (layer, 0, j)),
            pl.BlockSpec((None, d, tf), lambda i, j: (layer, 0, nf + j)),
            pl.BlockSpec((None, CONV_FFN_TAPS, tf), lambda i, j: (layer, 0, j)),
            pl.BlockSpec((None, 1, tf), lambda i, j: (layer, 0, j)),
            pl.BlockSpec((None, tf, d), lambda i, j: (layer, j, 0)),
            pl.BlockSpec((1, d), lambda i, j: (0, 0)),
        ],
        out_specs=pl.BlockSpec((tm, d), lambda i, j: (i, 0)),
        out_shape=jax.ShapeDtypeStruct((t, d), F32),
        scratch_shapes=[
            pltpu.VMEM((tm, d), BF16),
            pltpu.VMEM((tf // LANES, tm + SUBLANES, LANES), F32),
            pltpu.VMEM((nf, SUBLANES, tf), F32),
        ],
        compiler_params=_params(("arbitrary", "arbitrary")),
        name="ffn",
    )(x2, norm_w, w_up, w_up, conv_w, conv_b, w_down, final_w)


def _tile_plan(seq, d, dff):
    return dict(
        inproj_tm=_pick_tile(seq, 1024), inproj_tn=_pick_tile(d, 2048), inproj_row_blocks=1,
        mix_tm=_pick_tile(seq, 512),
        ffn_tm=_pick_tile(seq, 1024), ffn_tf=_pick_tile(dff, 512), ffn_row_blocks=2,
        heads_per_step=min(16, d // HEAD_DIM),
    )


def kernel(x, norm_mix_w, w_in, conv_qkv_w, a_log, dt_bias, gdn_norm_w, pool_w, pool_scale, w_out,
           norm_ffn_w, w_up, conv_ffn_w, conv_ffn_b, w_down, norm_final_w):
    bsz, seq, d = x.shape
    depth = w_in.shape[0]
    n_heads = d // HEAD_DIM
    dff = w_down.shape[1]
    assert seq % GDN_ROWS == 0 and d % HEAD_DIM == 0 and 2 * n_heads <= N_GATE_COLS
    assert w_in.shape[2] == 7 * d + 2 * n_heads
    plan = _tile_plan(seq, d, dff)
    x2 = x.reshape(bsz * seq, d)
    gate_lo = 4 * d
    lane_pad = N_GATE_COLS - 2 * n_heads

    w_t = jnp.swapaxes(w_in, 1, 2).astype(BF16)
    w_up_b, w_out_b, pool_w_b = (w.astype(BF16) for w in (w_up, w_out, pool_w))
    row3 = lambda a: a.reshape(depth, 1, a.shape[-1])
    al = row3(jnp.pad(a_log, ((0, 0), (n_heads, lane_pad))))
    dt = row3(jnp.pad(dt_bias, ((0, 0), (n_heads, lane_pad))))
    alr = jnp.broadcast_to(jnp.pad(a_log, ((0, 0), (n_heads, 0)))[:, :, None], (depth, 2 * n_heads, N_GATE_COLS))
    dtr = jnp.broadcast_to(jnp.pad(dt_bias, ((0, 0), (n_heads, 0)))[:, :, None], (depth, 2 * n_heads, N_GATE_COLS))
    norm_mix, norm_ffn, gdn_norm, pool_sc, ffn_b = (row3(a) for a in
                                                    (norm_mix_w, norm_ffn_w, gdn_norm_w, pool_scale, conv_ffn_b))

    for l in range(depth):
        proj, gates, gates_t = _inproj(x2, norm_mix, w_t, conv_qkv_w, layer=l, gate_row=gate_lo,
                                       n_gate_rows=2 * n_heads, seq=seq, tm=plan["inproj_tm"],
                                       tn=plan["inproj_tn"], row_blocks=plan["inproj_row_blocks"])
        ya = _gdn(proj, gates, gates_t, al, dt, alr, dtr, gdn_norm, layer=l,
                  batch=bsz, seq=seq, d_lin=d, heads_per_step=plan["heads_per_step"])
        x2 = _mix(proj, ya, x2, pool_w_b, pool_sc, w_out_b, layer=l, seq=seq, tm=plan["mix_tm"])
        x2 = _ffn(x2, norm_ffn, w_up_b, conv_ffn_w, ffn_b, w_down, norm_final_w.reshape(1, d), layer=l,
                  final_norm=(l == depth - 1), seq=seq, tm=plan["ffn_tm"], tf=plan["ffn_tf"],
                  row_blocks=plan["ffn_row_blocks"])
    return x2.reshape(bsz, seq, d)
```
